```python
import math
import jax, jax.numpy as jnp
from jax import lax
import numpy as np

D_MODEL = 1024
BATCH = 2
SEQ = 8192
DEPTH = 4
DEC_BATCH = 32
DEC_SEQ = 32
PAST_LEN = 2048

CHUNK = 64
N_META = 16
Q_BLOCK = 128
HEAD_DIM = 64
DIFF_HEADS = 4
FOX_HEADS = 8
DIFF_QK = DIFF_HEADS * 2 * HEAD_DIM
DIFF_V = DIFF_HEADS * 2 * HEAD_DIM
FOX_W = FOX_HEADS * HEAD_DIM
MIX = DIFF_V + FOX_W
SPLITS = (DIFF_QK, DIFF_QK, DIFF_V, FOX_W, FOX_W, FOX_W, FOX_HEADS, MIX)
IN_COLS = sum(SPLITS)
NORM_EPS = 1e-6
SUBLN_EPS = 1e-5
NEG = -1e30

kernel_name = "hymba_diff_fox_streaming_encoder"


def _rmsnorm(x, g, eps=NORM_EPS):
    xf = x.astype(jnp.float32)
    y = xf * lax.rsqrt(jnp.mean(xf * xf, axis=-1, keepdims=True) + eps)
    return (y * g.astype(jnp.float32)).astype(x.dtype)


def _chunk_id(idx):
    return jnp.where(idx < N_META, -1, (idx - N_META) // CHUNK)


def _project(h, w_in_l, b_f_l):
    B, T, _ = h.shape
    z = h @ w_in_l
    cuts = np.cumsum(SPLITS)[:-1].tolist()
    dq, dk, dv, fq, fk, fv, fl, gate = jnp.split(z, cuts, axis=-1)
    dq = dq.reshape(B, T, DIFF_HEADS, 2, HEAD_DIM)
    dk = dk.reshape(B, T, DIFF_HEADS, 2, HEAD_DIM)
    dv = dv.reshape(B, T, DIFF_HEADS, 2 * HEAD_DIM)
    fq = fq.reshape(B, T, FOX_HEADS, HEAD_DIM)
    fk = fk.reshape(B, T, FOX_HEADS, HEAD_DIM)
    fv = fv.reshape(B, T, FOX_HEADS, HEAD_DIM)
    logf = jax.nn.log_sigmoid((fl + b_f_l).astype(jnp.float32))
    return dq, dk, dv, fq, fk, fv, logf, gate


def _attend(qd, qf, qidx, Fq, Kd, Vd, Kf, Vf, kidx, Fk, lam, lam_init, subln_g):
    f32 = jnp.float32
    B, Tq = qd.shape[0], qd.shape[1]
    scale = HEAD_DIM ** -0.5
    slopes = 2.0 ** (-8.0 * jnp.arange(1, DIFF_HEADS + 1, dtype=f32) / DIFF_HEADS)
    dist = jnp.abs(qidx[:, None] - kidx[None, :]).astype(f32)
    sd = jnp.einsum('bqhme,bkhme->bhmqk', qd, Kd).astype(f32) * scale
    sd = sd - slopes[:, None, None, None] * dist
    chunk_ok = _chunk_id(kidx)[None, :] <= _chunk_id(qidx)[:, None]
    sd = jnp.where(chunk_ok, sd, NEG)
    pd = jax.nn.softmax(sd, axis=-1)
    ad = pd[:, :, 0] - lam * pd[:, :, 1]
    od = jnp.einsum('bhqk,bkhe->bqhe', ad.astype(Vd.dtype), Vd)
    od = _rmsnorm(od, subln_g, SUBLN_EPS) * (1.0 - lam_init)
    Fq_t = jnp.swapaxes(Fq, 1, 2)
    Fk_t = jnp.swapaxes(Fk, 1, 2)
    sf = jnp.einsum('bqhe,bkhe->bhqk', qf, Kf).astype(f32) * scale
    sf = sf + (Fq_t[:, :, :, None] - Fk_t[:, :, None, :])
    sf = jnp.where(kidx[None, :] <= qidx[:, None], sf, NEG)
    pf = jax.nn.softmax(sf, axis=-1)
    of = jnp.einsum('bhqk,bkhe->bqhe', pf.astype(Vf.dtype), Vf)
    return jnp.concatenate([od.reshape(B, Tq, DIFF_V), of.reshape(B, Tq, FOX_W)], axis=-1)


def _layer(x, prefix, n_lead, w_in_l, b_f_l, g_l, w_out_l, lam, lam_init, subln_g_l):
    B, T, _ = x.shape
    h = _rmsnorm(x, g_l)
    dq, dk, dv, fq, fk, fv, logf, gate = _project(h, w_in_l, b_f_l)
    if prefix is None:
        t_prev = 0
        Kd, Vd, Kf, Vf, LF = dk, dv, fk, fv, logf
    else:
        pdk, pdv, pfk, pfv, plf = prefix
        t_prev = pdk.shape[1]
        Kd = jnp.concatenate([pdk.astype(dk.dtype), dk], axis=1)
        Vd = jnp.concatenate([pdv.astype(dv.dtype), dv], axis=1)
        Kf = jnp.concatenate([pfk.astype(fk.dtype), fk], axis=1)
        Vf = jnp.concatenate([pfv.astype(fv.dtype), fv], axis=1)
        LF = jnp.concatenate([plf.astype(jnp.float32), logf], axis=1)
    F = jnp.cumsum(LF, axis=1)
    kidx = jnp.arange(t_prev + T)
    qidx = kidx[t_prev:]
    Fq = F[:, t_prev:]

    def attend_rows(a):
        q_d, q_f, q_i, F_q = a
        return _attend(q_d, q_f, q_i, F_q, Kd, Vd, Kf, Vf, kidx, F, lam, lam_init, subln_g_l)

    if n_lead >= T:
        out = attend_rows((dq, fq, qidx, Fq))
    else:
        lead = attend_rows((dq[:, :n_lead], fq[:, :n_lead], qidx[:n_lead], Fq[:, :n_lead]))
        nb = (T - n_lead) // Q_BLOCK

        def blk(a):
            return jnp.moveaxis(a.reshape(a.shape[0], nb, Q_BLOCK, *a.shape[2:]), 1, 0)

        rest = lax.map(attend_rows, (blk(dq[:, n_lead:]), blk(fq[:, n_lead:]),
                                     qidx[n_lead:].reshape(nb, Q_BLOCK), blk(Fq[:, n_lead:])))
        rest = jnp.moveaxis(rest, 0, 1).reshape(B, nb * Q_BLOCK, MIX)
        out = jnp.concatenate([lead, rest], axis=1)
    y = x + (out.astype(x.dtype) * jax.nn.silu(gate)) @ w_out_l
    return y, (dk, dv, fk, fv, logf)


def setup_inputs(seed: int = 0) -> dict:
    key = jax.random.key(seed)
    ks = jax.random.split(key, 20)
    L = N_META + PAST_LEN
    f32 = jnp.float32
    n = lambda k, s: jax.random.normal(k, s, f32)
    return {
        "x_prompt": n(ks[0], (BATCH, SEQ, D_MODEL)),
        "x_sample": n(ks[1], (DEC_BATCH, DEC_SEQ, D_MODEL)),
        "cache_diff_k": n(ks[2], (DEPTH, DEC_BATCH, L, DIFF_HEADS, 2, HEAD_DIM)),
        "cache_diff_v": n(ks[3], (DEPTH, DEC_BATCH, L, DIFF_HEADS, 2 * HEAD_DIM)),
        "cache_fox_k": n(ks[4], (DEPTH, DEC_BATCH, L, FOX_HEADS, HEAD_DIM)),
        "cache_fox_v": n(ks[5], (DEPTH, DEC_BATCH, L, FOX_HEADS, HEAD_DIM)),
        "cache_fox_logf": jax.nn.log_sigmoid(2.0 + 0.5 * n(ks[6], (DEPTH, DEC_BATCH, L, FOX_HEADS))),
        "meta_tokens": n(ks[7], (N_META, D_MODEL)),
        "w_in": n(ks[8], (DEPTH, D_MODEL, IN_COLS)) * D_MODEL ** -0.5,
        "b_forget": 2.0 + 0.1 * n(ks[9], (DEPTH, FOX_HEADS)),
        "norm_g": 1.0 + 0.01 * n(ks[10], (DEPTH, D_MODEL)),
        "w_out": n(ks[11], (DEPTH, MIX, D_MODEL)) * MIX ** -0.5,
        "lambda_q1": 0.1 * n(ks[12], (DEPTH, HEAD_DIM)),
        "lambda_k1": 0.1 * n(ks[13], (DEPTH, HEAD_DIM)),
        "lambda_q2": 0.1 * n(ks[14], (DEPTH, HEAD_DIM)),
        "lambda_k2": 0.1 * n(ks[15], (DEPTH, HEAD_DIM)),
        "subln_g": 1.0 + 0.01 * n(ks[16], (DEPTH, 2 * HEAD_DIM)),
        "final_norm_g": 1.0 + 0.01 * n(ks[17], (D_MODEL,)),
    }


def reference(x_prompt, x_sample, cache_diff_k, cache_diff_v, cache_fox_k, cache_fox_v, cache_fox_logf,
              meta_tokens, w_in, b_forget, norm_g, w_out, lambda_q1, lambda_k1, lambda_q2, lambda_k2,
              subln_g, final_norm_g):
    f32 = jnp.float32
    B = x_prompt.shape[0]
    meta = jnp.broadcast_to(meta_tokens[None].astype(x_prompt.dtype), (B, N_META, D_MODEL))
    hp = jnp.concatenate([meta, x_prompt], axis=1)
    hs = x_sample
    rows_p, rows_s = [], []
    for l in range(DEPTH):
        lam_init = 0.8 - 0.6 * math.exp(-0.3 * l)
        lam = (jnp.exp(jnp.sum(lambda_q1[l].astype(f32) * lambda_k1[l].astype(f32)))
               - jnp.exp(jnp.sum(lambda_q2[l].astype(f32) * lambda_k2[l].astype(f32))) + lam_init)
        hp, rp = _layer(hp, None, N_META, w_in[l], b_forget[l], norm_g[l], w_out[l], lam, lam_init, subln_g[l])
        prefix = (cache_diff_k[l], cache_diff_v[l], cache_fox_k[l], cache_fox_v[l], cache_fox_logf[l])
        hs, rs = _layer(hs, prefix, hs.shape[1], w_in[l], b_forget[l], norm_g[l], w_out[l], lam, lam_init, subln_g[l])
        rows_p.append(rp)
        rows_s.append(rs)
    y_prompt = _rmsnorm(hp[:, N_META:], final_norm_g)
    y_sample = _rmsnorm(hs, final_norm_g)
    p_diff_k = jnp.stack([r[0] for r in rows_p])
    p_diff_v = jnp.stack([r[1] for r in rows_p])
    p_fox_k = jnp.stack([r[2] for r in rows_p])
    p_fox_v = jnp.stack([r[3] for r in rows_p])
    p_fox_logf = jnp.stack([r[4] for r in rows_p])
    s_diff_k = jnp.stack([r[0] for r in rows_s])
    s_diff_v = jnp.stack([r[1] for r in rows_s])
    s_fox_k = jnp.stack([r[2] for r in rows_s])
    s_fox_v = jnp.stack([r[3] for r in rows_s])
    s_fox_logf = jnp.stack([r[4] for r in rows_s])
    return (y_prompt, y_sample, p_diff_k, p_diff_v, p_fox_k, p_fox_v, p_fox_logf,
            s_diff_k, s_diff_v, s_fox_k, s_fox_v, s_fox_logf)
```

```python
import functools
import math

import jax
import jax.numpy as jnp
from jax import lax
from jax.experimental import pallas as pl
from jax.experimental.pallas import tpu as pltpu

D_MODEL = 1024
N_META = 16
CHUNK_SHIFT = 6
HEAD_DIM = 64
DIFF_HEADS = 4
FOX_HEADS = 8
GROUPS = 4
GW = 2 * HEAD_DIM
HALF = GROUPS * GW
MIX = 2 * HALF
NORM_EPS = 1e-6
SUBLN_EPS = 1e-5
NEG = -1e30
LOG2E = 1.4426950408889634
QSCALE = HEAD_DIM ** -0.5 * LOG2E
CUM_CHUNK = 512
VMEM_LIMIT = 56 * 1024 * 1024

F32 = jnp.float32
BF16 = jnp.bfloat16


def _dot(a, b):
    return jnp.dot(a, b, preferred_element_type=F32)


def _dot_nt(a, b):
    return lax.dot_general(a, b, (((1,), (1,)), ((), ())), preferred_element_type=F32)


def _rms(x, g, eps):
    return (x * lax.rsqrt(jnp.mean(x * x, axis=-1, keepdims=True) + eps)) * g


def _proj_kernel(x_ref, g_ref, w_ref, wf_ref, bf_ref,
                 qd_ref, kd32_ref, kd16_ref, vd32_ref, vd16_ref,
                 qf_ref, kf32_ref, kf16_ref, vf32_ref, vf16_ref, lf_ref, gate_ref):
    hb = _rms(x_ref[...], g_ref[...], NORM_EPS).astype(BF16)

    def mm(c):
        return _dot(hb, w_ref[:, c * HALF:(c + 1) * HALF])

    qd_ref[...] = (mm(0) * QSCALE).astype(BF16)
    z = mm(1)
    kd32_ref[...] = z
    kd16_ref[...] = z.astype(BF16)
    z = mm(2)
    vd32_ref[...] = z
    vd16_ref[...] = z.astype(BF16)
    qf_ref[...] = (mm(3) * QSCALE).astype(BF16)
    z = mm(4)
    kf32_ref[...] = z
    kf16_ref[...] = z.astype(BF16)
    z = mm(5)
    vf32_ref[...] = z
    vf16_ref[...] = z.astype(BF16)
    gate_ref[:, 0:HALF] = mm(6)
    gate_ref[:, HALF:MIX] = mm(7)
    zf = _dot(hb, wf_ref[...]) + bf_ref[...]
    lf_ref[...] = jnp.minimum(zf, 0.0) - jnp.log(1.0 + jnp.exp(-jnp.abs(zf)))


def _project(x, g, w_main, w_f, b_f, tm):
    rows = x.shape[0]
    row = lambda i: (i, 0)
    fixed = lambda i: (0, 0)
    half16 = jax.ShapeDtypeStruct((rows, HALF), BF16)
    half32 = jax.ShapeDtypeStruct((rows, HALF), F32)
    out_shape = (half16, half32, half16, half32, half16,
                 half16, half32, half16, half32, half16,
                 jax.ShapeDtypeStruct((rows, 128), F32),
                 jax.ShapeDtypeStruct((rows, MIX), F32))
    out_specs = tuple(pl.BlockSpec((tm, s.shape[1]), row) for s in out_shape)
    return pl.pallas_call(
        _proj_kernel,
        out_shape=out_shape,
        grid=(rows // tm,),
        in_specs=[pl.BlockSpec((tm, D_MODEL), row),
                  pl.BlockSpec((1, D_MODEL), fixed),
                  pl.BlockSpec((D_MODEL, 8 * HALF), fixed),
                  pl.BlockSpec((D_MODEL, 128), fixed),
                  pl.BlockSpec((1, 128), fixed)],
        out_specs=out_specs,
        compiler_params=pltpu.CompilerParams(dimension_semantics=("arbitrary",),
                                             vmem_limit_bytes=VMEM_LIMIT),
        name="proj",
    )(x, g, w_main, w_f, b_f)


def _cumsum_kernel(x_ref, o_ref):
    rows, cols = x_ref.shape
    r = lax.broadcasted_iota(jnp.int32, (CUM_CHUNK, CUM_CHUNK), 0)
    c = lax.broadcasted_iota(jnp.int32, (CUM_CHUNK, CUM_CHUNK), 1)
    tri = (r <= c).astype(BF16)
    carry = jnp.zeros((rows, 1), F32)
    for n in range(cols // CUM_CHUNK):
        x = x_ref[:, n * CUM_CHUNK:(n + 1) * CUM_CHUNK]
        hi = x.astype(BF16)
        r1 = x - hi.astype(F32)
        mid = r1.astype(BF16)
        lo = (r1 - mid.astype(F32)).astype(BF16)
        y = (_dot(hi, tri) + _dot(mid, tri)) + _dot(lo, tri) + carry
        o_ref[:, n * CUM_CHUNK:(n + 1) * CUM_CHUNK] = y
        carry = y[:, CUM_CHUNK - 1:CUM_CHUNK]


def _cum_logf(logf):
    nb, t, h = logf.shape
    tpad = -(-t // CUM_CHUNK) * CUM_CHUNK
    x = jnp.swapaxes(logf, 1, 2).reshape(nb * h, t)
    x = jnp.pad(x, ((0, 0), (0, tpad - t)))
    f = pl.pallas_call(
        _cumsum_kernel,
        out_shape=jax.ShapeDtypeStruct((nb * h, tpad), F32),
        compiler_params=pltpu.CompilerParams(vmem_limit_bytes=VMEM_LIMIT),
        name="cumsum",
    )(x)
    return f.reshape(nb, GROUPS, 2, tpad)


def _lambda(lam_ref, lam_init):
    lv = lam_ref[...]
    a = jnp.sum(lv[0:1] * lv[1:2], axis=-1, keepdims=True)
    b = jnp.sum(lv[2:3] * lv[3:4], axis=-1, keepdims=True)
    return jnp.exp(a) - jnp.exp(b) + lam_init


def _split_q(q):
    lane = lax.broadcasted_iota(jnp.int32, q.shape, 1)
    zero = jnp.zeros_like(q)
    return jnp.where(lane < HEAD_DIM, q, zero), jnp.where(lane >= HEAD_DIM, q, zero)


def _finish(kind, accs, ls, lam, lam_init, g_sub):
    o1 = accs[0] / ls[0]
    o2 = accs[1] / ls[1]
    if kind == "diff":
        od = o1 - lam * o2
        return _rms(od, g_sub, SUBLN_EPS) * (1.0 - lam_init)
    lane = lax.broadcasted_iota(jnp.int32, o1.shape, 1)
    return jnp.where(lane < HEAD_DIM, o1, o2)


def _frames_attn_kernel(kind, tq, lam_init, *refs):
    if kind == "diff":
        sc_ref, lam_ref, q_ref, k_ref, v_ref, mk_ref, mv_ref, g_ref, o_ref = refs
    else:
        q_ref, k_ref, v_ref, mk_ref, mv_ref, fk_ref, fm_ref, o_ref = refs
    gidx = pl.program_id(1)
    qi = pl.program_id(2)
    qs = _split_q(q_ref[0])
    if kind == "diff":
        slope = sc_ref[gidx]

    mk = mk_ref[0]
    mv = mv_ref[0]
    if kind == "diff":
        kpos = lax.broadcasted_iota(jnp.int32, (1, N_META), 1) - (N_META + qi * tq)
        bias_m = [slope * kpos.astype(F32)] * 2
    else:
        fm = fm_ref[0, 0]
        bias_m = [-LOG2E * fm[0:1], -LOG2E * fm[1:2]]
    carry = []
    for i in range(2):
        s = _dot_nt(qs[i], mk) + bias_m[i]
        m = jnp.max(s, axis=-1, keepdims=True)
        p = jnp.exp2(s - m)
        carry += [m, jnp.sum(p, axis=-1, keepdims=True), _dot(p.astype(BF16), mv)]

    def step(n, carry, diag):
        ks = pl.multiple_of(n * tq, tq)
        kb = k_ref[0, pl.ds(ks, tq), :]
        vb = v_ref[0, pl.ds(ks, tq), :]
        if diag:
            qr = lax.broadcasted_iota(jnp.int32, (tq, tq), 0)
            kr = lax.broadcasted_iota(jnp.int32, (tq, tq), 1)
        if kind == "diff":
            if diag:
                bias = [slope * (2 * jnp.minimum(qr, kr) - kr).astype(F32)] * 2
                mask = (kr >> CHUNK_SHIFT) <= (qr >> CHUNK_SHIFT)
            else:
                kpos = lax.broadcasted_iota(jnp.int32, (1, tq), 1) + (n - qi) * tq
                bias = [slope * kpos.astype(F32)] * 2
        else:
            fk = fk_ref[0, 0, :, pl.ds(ks, tq)]
            bias = [-LOG2E * fk[0:1], -LOG2E * fk[1:2]]
            if diag:
                mask = kr <= qr
        out = []
        for i in range(2):
            m, l, acc = carry[3 * i:3 * i + 3]
            s = _dot_nt(qs[i], kb) + bias[i]
            if diag:
                s = jnp.where(mask, s, NEG)
            m_new = jnp.maximum(m, jnp.max(s, axis=-1, keepdims=True))
            alpha = jnp.exp2(m - m_new)
            p = jnp.exp2(s - m_new)
            l = alpha * l + jnp.sum(p, axis=-1, keepdims=True)
            acc = alpha * acc + _dot(p.astype(BF16), vb)
            out += [m_new, l, acc]
        return tuple(out)

    carry = lax.fori_loop(0, qi, lambda n, c: step(n, c, False), tuple(carry))
    carry = step(qi, carry, True)
    if kind == "diff":
        lam = _lambda(lam_ref, lam_init)
        g_sub = g_ref[...]
    else:
        lam = g_sub = None
    o_ref[0] = _finish(kind, (carry[2], carry[5]), (carry[1], carry[4]), lam, lam_init, g_sub)


def _frames_attention(kind, tq, lam_init, q, k, v, mk, mv, extra):
    nb, t, _ = q.shape
    qspec = pl.BlockSpec((1, tq, GW), lambda b, g, i: (b, i, g))
    kvspec = pl.BlockSpec((1, t, GW), lambda b, g, i: (b, 0, g))
    mspec = pl.BlockSpec((1, N_META, GW), lambda b, g, i: (b, 0, g))
    if kind == "diff":
        slopes, lam_vecs, g_sub = extra
        args = (slopes, lam_vecs, q, k, v, mk, mv, g_sub)
        in_specs = [pl.BlockSpec(memory_space=pltpu.SMEM),
                    pl.BlockSpec((4, HEAD_DIM), lambda b, g, i: (0, 0)),
                    qspec, kvspec, kvspec, mspec, mspec,
                    pl.BlockSpec((1, GW), lambda b, g, i: (0, 0))]
    else:
        f_frames, f_meta = extra
        args = (q, k, v, mk, mv, f_frames, f_meta)
        in_specs = [qspec, kvspec, kvspec, mspec, mspec,
                    pl.BlockSpec((1, 1, 2, t), lambda b, g, i: (b, g, 0, 0)),
                    pl.BlockSpec((1, 1, 2, N_META), lambda b, g, i: (b, g, 0, 0))]
    return pl.pallas_call(
        functools.partial(_frames_attn_kernel, kind, tq, lam_init),
        out_shape=jax.ShapeDtypeStruct((nb, t, HALF), F32),
        grid=(nb, GROUPS, t // tq),
        in_specs=in_specs,
        out_specs=pl.BlockSpec((1, tq, GW), lambda b, g, i: (b, i, g)),
        compiler_params=pltpu.CompilerParams(
            dimension_semantics=("arbitrary", "arbitrary", "arbitrary"),
            vmem_limit_bytes=VMEM_LIMIT),
        name="frames_attn_" + kind,
    )(*args)


def _small_attn_kernel(kind, tq, lc, ln, lam_init, *refs):
    refs = list(refs)
    o_ref = refs.pop()
    if kind == "diff":
        sc_ref, lam_ref = refs[0:2]
        refs = refs[2:]
    q_ref = refs.pop(0)
    if lc:
        kc_ref, vc_ref = refs[0:2]
        refs = refs[2:]
    kn_ref, vn_ref = refs[0:2]
    refs = refs[2:]
    if kind == "diff":
        g_ref, = refs
        lam = _lambda(lam_ref, lam_init)
        g_sub = g_ref[...]
    else:
        if lc:
            fc_ref = refs.pop(0)
        fn_ref, = refs
        lam = g_sub = None

    lc_main = (lc // 128) * 128
    spans = [(0, lc_main), (lc_main, lc)] if lc else []
    spans = [sp for sp in spans if sp[1] > sp[0]]
    qpos = lc + lax.broadcasted_iota(jnp.int32, (tq, 1), 0)

    for g in range(GROUPS):
        cols = slice(g * GW, (g + 1) * GW)
        qs = _split_q(q_ref[0, :, cols])
        segs = []
        for a, b in spans:
            f = fc_ref[0, g, :, a:b] if kind == "fox" else None
            segs.append((kc_ref[0, a:b, cols].astype(BF16), vc_ref[0, a:b, cols].astype(BF16), a, f))
        f = fn_ref[0, g] if kind == "fox" else None
        segs.append((kn_ref[0, :, cols], vn_ref[0, :, cols], lc, f))

        accs, ls = [], []
        for i in range(2):
            ss = []
            for kb, _, start, f in segs:
                kpos = start + lax.broadcasted_iota(jnp.int32, (1, kb.shape[0]), 1)
                s = _dot_nt(qs[i], kb)
                if kind == "diff":
                    s = s - sc_ref[g] * jnp.abs(qpos - kpos).astype(F32)
                    mask = ((kpos - N_META) >> CHUNK_SHIFT) <= ((qpos - N_META) >> CHUNK_SHIFT)
                else:
                    s = s - LOG2E * f[i:i + 1]
                    mask = kpos <= qpos
                ss.append(jnp.where(mask, s, NEG))
            m = functools.reduce(jnp.maximum, [jnp.max(s, axis=-1, keepdims=True) for s in ss])
            ps = [jnp.exp2(s - m) for s in ss]
            ls.append(functools.reduce(jnp.add, [jnp.sum(p, axis=-1, keepdims=True) for p in ps]))
            accs.append(functools.reduce(
                jnp.add, [_dot(p.astype(BF16), seg[1]) for p, seg in zip(ps, segs)]))
        o_ref[0, :, cols] = _finish(kind, accs, ls, lam, lam_init, g_sub)


def _small_attention(kind, lam_init, q, kc, vc, kn, vn, extra):
    nb, tq, _ = q.shape
    lc = 0 if kc is None else kc.shape[1]
    ln = kn.shape[1]
    per_b = lambda rows: pl.BlockSpec((1, rows, HALF), lambda b: (b, 0, 0))
    args, in_specs = [], []
    if kind == "diff":
        slopes, lam_vecs, g_sub = extra
        args += [slopes, lam_vecs]
        in_specs += [pl.BlockSpec(memory_space=pltpu.SMEM),
                     pl.BlockSpec((4, HEAD_DIM), lambda b: (0, 0))]
    args.append(q)
    in_specs.append(per_b(tq))
    if lc:
        args += [kc, vc]
        in_specs += [per_b(lc), per_b(lc)]
    args += [kn, vn]
    in_specs += [per_b(ln), per_b(ln)]
    if kind == "diff":
        args.append(g_sub)
        in_specs.append(pl.BlockSpec((1, GW), lambda b: (0, 0)))
    else:
        f_cache, f_new = extra
        if lc:
            args.append(f_cache)
            in_specs.append(pl.BlockSpec((1, GROUPS, 2, lc), lambda b: (b, 0, 0, 0)))
        args.append(f_new)
        in_specs.append(pl.BlockSpec((1, GROUPS, 2, ln), lambda b: (b, 0, 0, 0)))
    return pl.pallas_call(
        functools.partial(_small_attn_kernel, kind, tq, lc, ln, lam_init),
        out_shape=jax.ShapeDtypeStruct((nb, tq, HALF), F32),
        grid=(nb,),
        in_specs=in_specs,
        out_specs=per_b(tq),
        compiler_params=pltpu.CompilerParams(dimension_semantics=("arbitrary",),
                                             vmem_limit_bytes=VMEM_LIMIT),
        name="small_attn_" + kind,
    )(*args)


def _outproj_kernel(final, ad_ref, af_ref, gate_ref, x_ref, w_ref, gf_ref, y_ref):
    gt = gate_ref[...]
    sg = gt / (1.0 + jnp.exp(-gt))
    ud = (ad_ref[...] * sg[:, 0:HALF]).astype(BF16)
    uf = (af_ref[...] * sg[:, HALF:MIX]).astype(BF16)
    y = x_ref[...] + (_dot(ud, w_ref[0:HALF, :]) + _dot(uf, w_ref[HALF:MIX, :]))
    if final:
        y = _rms(y, gf_ref[...], NORM_EPS)
    y_ref[...] = y


def _out_project(final, ad, af, gate, x, w_out, g_final, tm):
    rows = x.shape[0]
    row = lambda i: (i, 0)
    fixed = lambda i: (0, 0)
    return pl.pallas_call(
        functools.partial(_outproj_kernel, final),
        out_shape=jax.ShapeDtypeStruct((rows, D_MODEL), F32),
        grid=(rows // tm,),
        in_specs=[pl.BlockSpec((tm, HALF), row), pl.BlockSpec((tm, HALF), row),
                  pl.BlockSpec((tm, MIX), row), pl.BlockSpec((tm, D_MODEL), row),
                  pl.BlockSpec((MIX, D_MODEL), fixed), pl.BlockSpec((1, D_MODEL), fixed)],
        out_specs=pl.BlockSpec((tm, D_MODEL), row),
        compiler_params=pltpu.CompilerParams(dimension_semantics=("arbitrary",),
                                             vmem_limit_bytes=VMEM_LIMIT),
        name="outproj",
    )(ad, af, gate, x, w_out, g_final)


def _row_tile(rows):
    for tm in (256, 128, 64, 32, 16, 8):
        if rows % tm == 0:
            return tm
    raise ValueError(f"unsupported row count {rows}")


def kernel(x_prompt, x_sample, cache_diff_k, cache_diff_v, cache_fox_k, cache_fox_v, cache_fox_logf,
           meta_tokens, w_in, b_forget, norm_g, w_out, lambda_q1, lambda_k1, lambda_q2, lambda_k2,
           subln_g, final_norm_g):
    depth = w_in.shape[0]
    nb_p, seq, _ = x_prompt.shape
    nb_s, seq_s, _ = x_sample.shape
    lc = cache_diff_k.shape[2]
    tq = 512

    fl0 = 6 * HALF
    w_main = jnp.concatenate([w_in[:, :, :fl0], w_in[:, :, fl0 + FOX_HEADS:]], axis=-1).astype(BF16)
    w_f = jnp.pad(w_in[:, :, fl0:fl0 + FOX_HEADS], ((0, 0), (0, 0), (0, 128 - FOX_HEADS))).astype(BF16)
    b_f = jnp.pad(b_forget.astype(F32), ((0, 0), (0, 128 - FOX_HEADS)))
    w_o = w_out.astype(BF16)
    slopes = (2.0 ** (-8.0 * jnp.arange(1, DIFF_HEADS + 1, dtype=F32) / DIFF_HEADS)) * LOG2E
    g_final = final_norm_g.astype(F32).reshape(1, D_MODEL)

    xf = x_prompt.reshape(nb_p * seq, D_MODEL)
    xm = jnp.broadcast_to(meta_tokens[None].astype(x_prompt.dtype),
                          (nb_p, N_META, D_MODEL)).reshape(nb_p * N_META, D_MODEL)
    xs = x_sample.reshape(nb_s * seq_s, D_MODEL)

    rows_p = [[] for _ in range(5)]
    rows_s = [[] for _ in range(5)]
    for l in range(depth):
        lam_init = 0.8 - 0.6 * math.exp(-0.3 * l)
        final = l == depth - 1
        g_l = norm_g[l].astype(F32).reshape(1, D_MODEL)
        lam_vecs = jnp.stack([lambda_q1[l], lambda_k1[l], lambda_q2[l], lambda_k2[l]]).astype(F32)
        g_sub = subln_g[l].astype(F32).reshape(1, GW)
        diff_extra = (slopes, lam_vecs, g_sub)

        def project(x):
            return _project(x, g_l, w_main[l], w_f[l], b_f[l:l + 1], _row_tile(x.shape[0]))

        def out_project(ad, af, gate, x):
            return _out_project(final, ad.reshape(-1, HALF), af.reshape(-1, HALF), gate, x, w_o[l],
                                g_final, _row_tile(x.shape[0]))

        pf = project(xf)
        pm = project(xm)
        shape3 = lambda a, nb: a.reshape(nb, -1, a.shape[-1])
        (qd, kd32, kd16, vd32, vd16, qf, kf32, kf16, vf32, vf16, lf, gate) = [shape3(a, nb_p) for a in pf]
        (mqd, mkd32, mkd16, mvd32, mvd16, mqf, mkf32, mkf16, mvf32, mvf16, mlf, mgate) = [
            shape3(a, nb_p) for a in pm]
        logf_p = jnp.concatenate([mlf[:, :, :FOX_HEADS], lf[:, :, :FOX_HEADS]], axis=1)
        f_all = _cum_logf(logf_p)
        f_meta = f_all[..., :N_META]
        f_frames = f_all[..., N_META:N_META + seq]

        ad = _frames_attention("diff", tq, lam_init, qd, kd16, vd16, mkd16, mvd16, diff_extra)
        af = _frames_attention("fox", tq, lam_init, qf, kf16, vf16, mkf16, mvf16, (f_frames, f_meta))
        mad = _small_attention("diff", lam_init, mqd, None, None, mkd16, mvd16, diff_extra)
        maf = _small_attention("fox", lam_init, mqf, None, None, mkf16, mvf16, (None, f_meta))
        xf = out_project(ad, af, gate.reshape(-1, MIX), xf)
        if not final:
            xm = out_project(mad, maf, mgate.reshape(-1, MIX), xm)
        for dst, m_rows, f_rows in zip(rows_p, (mkd32, mvd32, mkf32, mvf32), (kd32, vd32, kf32, vf32)):
            dst.append(jnp.concatenate([m_rows, f_rows], axis=1))
        rows_p[4].append(logf_p)

        ps = project(xs)
        (sqd, skd32, skd16, svd32, svd16, sqf, skf32, skf16, svf32, svf16, slf, sgate) = [
            shape3(a, nb_s) for a in ps]
        logf_s = slf[:, :, :FOX_HEADS]
        f_s = _cum_logf(jnp.concatenate([cache_fox_logf[l].astype(F32), logf_s], axis=1))
        ckd = cache_diff_k[l].reshape(nb_s, lc, HALF)
        cvd = cache_diff_v[l].reshape(nb_s, lc, HALF)
        ckf = cache_fox_k[l].reshape(nb_s, lc, HALF)
        cvf = cache_fox_v[l].reshape(nb_s, lc, HALF)
        sad = _small_attention("diff", lam_init, sqd, ckd, cvd, skd16, svd16, diff_extra)
        saf = _small_attention("fox", lam_init, sqf, ckf, cvf, skf16, svf16,
                               (f_s[..., :lc], f_s[..., lc:lc + seq_s]))
        xs = out_project(sad, saf, sgate.reshape(-1, MIX), xs)
        for dst, a in zip(rows_s, (skd32, svd32, skf32, svf32, logf_s)):
            dst.append(a)

    y_prompt = xf.reshape(nb_p, seq, D_MODEL)
    y_sample = xs.reshape(nb_s, seq_s, D_MODEL)
    tp = N_META + seq
    p_diff_k = jnp.stack(rows_p[0]).reshape(depth, nb_p, tp, DIFF_HEADS, 2, HEAD_DIM)
    p_diff_v = jnp.stack(rows_p[1]).reshape(depth, nb_p, tp, DIFF_HEADS, 2 * HEAD_DIM)
    p_fox_k = jnp.stack(rows_p[2]).reshape(depth, nb_p, tp, FOX_HEADS, HEAD_DIM)
    p_fox_v = jnp.stack(rows_p[3]).reshape(depth, nb_p, tp, FOX_HEADS, HEAD_DIM)
    p_fox_logf = jnp.stack(rows_p[4])
    s_diff_k = jnp.stack(rows_s[0]).reshape(depth, nb_s, seq_s, DIFF_HEADS, 2, HEAD_DIM)
    s_diff_v = jnp.stack(rows_s[1]).reshape(depth, nb_s, seq_s, DIFF_HEADS, 2 * HEAD_DIM)
    s_fox_k = jnp.stack(rows_s[2]).reshape(depth, nb_s, seq_s, FOX_HEADS, HEAD_DIM)
    s_fox_v = jnp.stack(rows_s[3]).reshape(depth, nb_s, seq_s, FOX_HEADS, HEAD_DIM)
    s_fox_logf = jnp.stack(rows_s[4])
    return (y_prompt, y_sample, p_diff_k, p_diff_v, p_fox_k, p_fox_v, p_fox_logf,
            s_diff_k, s_diff_v, s_fox_k, s_fox_v, s_fox_logf)
```

```python
import functools
import math

import jax
import jax.numpy as jnp
from jax import lax
from jax.experimental import pallas as pl
from jax.experimental.pallas import tpu as pltpu

D_MODEL = 1024
N_META = 16
CHUNK_SHIFT = 6
HEAD_DIM = 64
DIFF_HEADS = 4
FOX_HEADS = 8
GROUPS = 4
GW = 2 * HEAD_DIM
HALF = GROUPS * GW
MIX = 2 * HALF
NORM_EPS = 1e-6
SUBLN_EPS = 1e-5
NEG = -1e30
LOG2E = 1.4426950408889634
QSCALE = HEAD_DIM ** -0.5 * LOG2E
LANES = 128
CUM_CHUNK = 512
ATTN_BLOCK = 512
ROW_TILE = 256
VMEM_LIMIT = 56 * 1024 * 1024

F32 = jnp.float32
BF16 = jnp.bfloat16


def _dot(a, b):
    return jnp.dot(a, b, preferred_element_type=F32)


def _dot_nt(a, b):
    return lax.dot_general(a, b, (((1,), (1,)), ((), ())), preferred_element_type=F32)


def _rms(x, g, eps):
    return (x * lax.rsqrt(jnp.mean(x * x, axis=-1, keepdims=True) + eps)) * g


def _log_sigmoid(z):
    return jnp.minimum(z, 0.0) - jnp.log(1.0 + jnp.exp(-jnp.abs(z)))


def _proj_rows_kernel(x_ref, g_ref, w_ref, wf_ref, bf_ref,
                      qd_ref, kd32_ref, kd16_ref, vd32_ref, vd16_ref,
                      qf_ref, kf32_ref, kf16_ref, vf32_ref, vf16_ref, lf_ref, gate_ref):
    hb = _rms(x_ref[...], g_ref[...], NORM_EPS).astype(BF16)

    def mm(c):
        return _dot(hb, w_ref[:, c * HALF:(c + 1) * HALF])

    qd_ref[...] = (mm(0) * QSCALE).astype(BF16)
    for c, r32, r16 in ((1, kd32_ref, kd16_ref), (2, vd32_ref, vd16_ref),
                        (4, kf32_ref, kf16_ref), (5, vf32_ref, vf16_ref)):
        z = mm(c)
        r32[...] = z
        r16[...] = z.astype(BF16)
    qf_ref[...] = (mm(3) * QSCALE).astype(BF16)
    gate_ref[:, 0:HALF] = mm(6)
    gate_ref[:, HALF:MIX] = mm(7)
    lf_ref[...] = _log_sigmoid(_dot(hb, wf_ref[...]) + bf_ref[...])


def _project_rows(x, g, w_rows, w_f, b_f):
    rows = x.shape[0]
    tm = math.gcd(rows, ROW_TILE)
    row = lambda i: (i, 0)
    fixed = lambda i: (0, 0)
    half16 = jax.ShapeDtypeStruct((rows, HALF), BF16)
    half32 = jax.ShapeDtypeStruct((rows, HALF), F32)
    out_shape = (half16, half32, half16, half32, half16,
                 half16, half32, half16, half32, half16,
                 jax.ShapeDtypeStruct((rows, LANES), F32),
                 jax.ShapeDtypeStruct((rows, MIX), F32))
    return pl.pallas_call(
        _proj_rows_kernel,
        out_shape=out_shape,
        grid=(rows // tm,),
        in_specs=[pl.BlockSpec((tm, D_MODEL), row),
                  pl.BlockSpec((1, D_MODEL), fixed),
                  pl.BlockSpec((D_MODEL, 8 * HALF), fixed),
                  pl.BlockSpec((D_MODEL, LANES), fixed),
                  pl.BlockSpec((1, LANES), fixed)],
        out_specs=tuple(pl.BlockSpec((tm, s.shape[1]), row) for s in out_shape),
        compiler_params=pltpu.CompilerParams(dimension_semantics=("arbitrary",),
                                             vmem_limit_bytes=VMEM_LIMIT),
        name="proj_rows",
    )(x, g, w_rows, w_f, b_f)


def _proj_frames_kernel(x_ref, g_ref, w_ref, wt_ref, wft_ref, bft_ref,
                        qd_ref, kdt16_ref, vd16_ref, qf_ref, kft16_ref, vf16_ref,
                        kdt32_ref, vd32_ref, kft32_ref, vft32_ref, lft_ref, gate_ref):
    hb = _rms(x_ref[0], g_ref[...], NORM_EPS).astype(BF16)

    def mm(c):
        return _dot(hb, w_ref[:, c * HALF:(c + 1) * HALF])

    def mm_t(c):
        return _dot_nt(wt_ref[c * HALF:(c + 1) * HALF, :], hb)

    qd_ref[0] = (mm(0) * QSCALE).astype(BF16)
    z = mm(1)
    vd32_ref[0] = z
    vd16_ref[0] = z.astype(BF16)
    qf_ref[0] = (mm(2) * QSCALE).astype(BF16)
    vf16_ref[0] = mm(3).astype(BF16)
    gate_ref[0, :, 0:HALF] = mm(4)
    gate_ref[0, :, HALF:MIX] = mm(5)
    z = mm_t(0)
    kdt32_ref[0] = z
    kdt16_ref[0] = z.astype(BF16)
    z = mm_t(1)
    kft32_ref[0] = z
    kft16_ref[0] = z.astype(BF16)
    vft32_ref[0] = mm_t(2)
    zf = _dot_nt(wft_ref[...], hb)
    lft_ref[0] = _log_sigmoid(zf[0:FOX_HEADS] + bft_ref[...])


def _project_frames(x, g, w_cols, w_t, w_ft, b_ft):
    nb, t, _ = x.shape
    tm = ROW_TILE
    rows = lambda b, i: (b, i, 0)
    cols = lambda b, i: (b, 0, i)
    fixed = lambda b, i: (0, 0)
    rm16 = (jax.ShapeDtypeStruct((nb, t, HALF), BF16), pl.BlockSpec((1, tm, HALF), rows))
    rm32 = (jax.ShapeDtypeStruct((nb, t, HALF), F32), pl.BlockSpec((1, tm, HALF), rows))
    fm16 = (jax.ShapeDtypeStruct((nb, HALF, t), BF16), pl.BlockSpec((1, HALF, tm), cols))
    fm32 = (jax.ShapeDtypeStruct((nb, HALF, t), F32), pl.BlockSpec((1, HALF, tm), cols))
    lft = (jax.ShapeDtypeStruct((nb, FOX_HEADS, t), F32), pl.BlockSpec((1, FOX_HEADS, tm), cols))
    gate = (jax.ShapeDtypeStruct((nb, t, MIX), F32), pl.BlockSpec((1, tm, MIX), rows))
    outs = (rm16, fm16, rm16, rm16, fm16, rm16, fm32, rm32, fm32, fm32, lft, gate)
    return pl.pallas_call(
        _proj_frames_kernel,
        out_shape=tuple(o[0] for o in outs),
        grid=(nb, t // tm),
        in_specs=[pl.BlockSpec((1, tm, D_MODEL), rows),
                  pl.BlockSpec((1, D_MODEL), fixed),
                  pl.BlockSpec((D_MODEL, 6 * HALF), fixed),
                  pl.BlockSpec((3 * HALF, D_MODEL), fixed),
                  pl.BlockSpec((2 * FOX_HEADS, D_MODEL), fixed),
                  pl.BlockSpec((FOX_HEADS, 1), fixed)],
        out_specs=tuple(o[1] for o in outs),
        compiler_params=pltpu.CompilerParams(dimension_semantics=("arbitrary", "arbitrary"),
                                             vmem_limit_bytes=VMEM_LIMIT),
        name="proj_frames",
    )(x, g, w_cols, w_t, w_ft, b_ft)


def _cumsum_kernel(x_ref, o_ref):
    rows, cols = x_ref.shape
    r = lax.broadcasted_iota(jnp.int32, (CUM_CHUNK, CUM_CHUNK), 0)
    c = lax.broadcasted_iota(jnp.int32, (CUM_CHUNK, CUM_CHUNK), 1)
    tri = (r <= c).astype(BF16)
    carry = jnp.zeros((rows, 1), F32)
    for n in range(cols // CUM_CHUNK):
        x = x_ref[:, n * CUM_CHUNK:(n + 1) * CUM_CHUNK]
        hi = x.astype(BF16)
        r1 = x - hi.astype(F32)
        mid = r1.astype(BF16)
        lo = (r1 - mid.astype(F32)).astype(BF16)
        y = (_dot(hi, tri) + _dot(mid, tri)) + _dot(lo, tri) + carry
        o_ref[:, n * CUM_CHUNK:(n + 1) * CUM_CHUNK] = y
        carry = y[:, CUM_CHUNK - 1:CUM_CHUNK]


def _cum_logf(logf_t):
    nb, h, t = logf_t.shape
    tpad = -(-t // CUM_CHUNK) * CUM_CHUNK
    x = jnp.pad(logf_t.reshape(nb * h, t), ((0, 0), (0, tpad - t)))
    f = pl.pallas_call(
        _cumsum_kernel,
        out_shape=jax.ShapeDtypeStruct((nb * h, tpad), F32),
        compiler_params=pltpu.CompilerParams(vmem_limit_bytes=VMEM_LIMIT),
        name="cumsum",
    )(x)
    return f.reshape(nb, GROUPS, 2, tpad)


def _lambda(lam_ref, lam_init):
    lv = lam_ref[...]
    a = jnp.sum(lv[0:1] * lv[1:2], axis=-1, keepdims=True)
    b = jnp.sum(lv[2:3] * lv[3:4], axis=-1, keepdims=True)
    return jnp.exp(a) - jnp.exp(b) + lam_init


def _split_lanes(x, fill):
    lane = lax.broadcasted_iota(jnp.int32, x.shape, 1)
    f = jnp.full_like(x, fill)
    return jnp.where(lane < HEAD_DIM, x, f), jnp.where(lane >= HEAD_DIM, x, f)


def _diff_out(o1, o2, lam, lam_init, g_sub):
    return _rms(o1 - lam * o2, g_sub, SUBLN_EPS) * (1.0 - lam_init)


def _frames_attn_kernel(kind, tq, lam_init, *refs):
    if kind == "diff":
        sc_ref, lam_ref, q_ref, kt_ref, v_ref, mk_ref, mv_ref, g_ref, o_ref, sa_ref, sb_ref, acc_ref = refs
    else:
        q_ref, kt_ref, v_ref, mk_ref, mv_ref, fk_ref, fm_ref, o_ref, sa_ref, sb_ref, acc_ref = refs
    gidx = pl.program_id(1)
    qi = pl.program_id(2)
    qs = _split_lanes(q_ref[0], 0)
    s_refs = (sa_ref, sb_ref)
    if kind == "diff":
        slope = sc_ref[gidx]

    def values(v):
        if kind == "diff":
            ve = jnp.concatenate([v, jnp.ones_like(v)], axis=1)
            return ve, ve
        return _split_lanes(v, 1)

    def produce(n, slot, diag):
        ks = pl.multiple_of(n * tq, tq)
        ktb = kt_ref[0, :, pl.ds(ks, tq)]
        if diag:
            qr = lax.broadcasted_iota(jnp.int32, (tq, tq), 0)
            kr = lax.broadcasted_iota(jnp.int32, (tq, tq), 1)
        if kind == "diff":
            if diag:
                bias = [slope * (2 * jnp.minimum(qr, kr) - kr).astype(F32)] * 2
                mask = (kr >> CHUNK_SHIFT) <= (qr >> CHUNK_SHIFT)
            else:
                kpos = lax.broadcasted_iota(jnp.int32, (1, tq), 1) + (n - qi) * tq
                bias = [slope * kpos.astype(F32)] * 2
        else:
            fk = fk_ref[0, 0, :, pl.ds(ks, tq)]
            bias = [-LOG2E * fk[0:1], -LOG2E * fk[1:2]]
            if diag:
                mask = kr <= qr
        bms = []
        for i in range(2):
            s = _dot(qs[i], ktb) + bias[i]
            if diag:
                s = jnp.where(mask, s, NEG)
            s_refs[slot][i] = s
            bms.append(jnp.max(s, axis=-1, keepdims=True))
        return tuple(bms)

    def consume(n, slot, ms, bms):
        ks = pl.multiple_of(n * tq, tq)
        vs = values(v_ref[0, pl.ds(ks, tq), :])
        out = []
        for i in range(2):
            m_new = jnp.maximum(ms[i], bms[i])
            alpha = jnp.exp2(ms[i] - m_new)
            p = jnp.exp2(s_refs[slot][i] - m_new)
            acc_ref[i] = alpha * acc_ref[i] + _dot(p.astype(BF16), vs[i])
            out.append(m_new)
        return tuple(out)

    mk = mk_ref[0]
    mvs = values(mv_ref[0])
    if kind == "diff":
        kpos = lax.broadcasted_iota(jnp.int32, (1, N_META), 1) - (N_META + qi * tq)
        bias_m = [slope * kpos.astype(F32)] * 2
    else:
        fm = fm_ref[0, 0]
        bias_m = [-LOG2E * fm[0:1], -LOG2E * fm[1:2]]
    ms = []
    for i in range(2):
        s = _dot_nt(qs[i], mk) + bias_m[i]
        m = jnp.max(s, axis=-1, keepdims=True)
        acc_ref[i] = _dot(jnp.exp2(s - m).astype(BF16), mvs[i])
        ms.append(m)
    ms = tuple(ms)

    bm0 = produce(qi, 0, True)

    def pair(t, c):
        ms, bm0 = c[:2], c[2:]
        bm1 = produce(2 * t, 1, False)
        ms = consume(jnp.where(t == 0, qi, 2 * t - 1), 0, ms, bm0)
        bm0 = produce(2 * t + 1, 0, False)
        ms = consume(2 * t, 1, ms, bm1)
        return ms + bm0

    c = lax.fori_loop(0, qi // 2, pair, ms + bm0)
    ms, bm0 = c[:2], c[2:]
    pending = jnp.where(qi < 2, qi, 2 * (qi // 2) - 1)

    def odd(ms, bm0):
        bm1 = produce(qi - 1, 1, False)
        ms = consume(pending, 0, ms, bm0)
        return consume(qi - 1, 1, ms, bm1)

    def even(ms, bm0):
        return consume(pending, 0, ms, bm0)

    lax.cond(qi % 2 == 1, odd, even, ms, bm0)

    a = acc_ref[0]
    b = acc_ref[1]
    if kind == "diff":
        o_ref[0] = _diff_out(a[:, :GW] / a[:, GW:], b[:, :GW] / b[:, GW:],
                             _lambda(lam_ref, lam_init), lam_init, g_ref[...])
    else:
        lane = lax.broadcasted_iota(jnp.int32, a.shape, 1)
        o_ref[0] = jnp.where(lane < HEAD_DIM, a / pltpu.roll(a, HEAD_DIM, 1), b / pltpu.roll(b, HEAD_DIM, 1))


def _frames_attention(kind, lam_init, q, kt, v, mk, mv, extra):
    nb, t, _ = q.shape
    tq = ATTN_BLOCK
    idx = lambda b, g, i: (b, i, g)
    rows_g = lambda b, g, i: (b, 0, g)
    feat_g = lambda b, g, i: (b, g, 0)
    fixed = lambda b, g, i: (0, 0)
    qspec = pl.BlockSpec((1, tq, GW), idx)
    common = [qspec, pl.BlockSpec((1, GW, t), feat_g), pl.BlockSpec((1, t, GW), rows_g),
              pl.BlockSpec((1, N_META, GW), rows_g), pl.BlockSpec((1, N_META, GW), rows_g)]
    if kind == "diff":
        slopes, lam_vecs, g_sub = extra
        args = (slopes, lam_vecs, q, kt, v, mk, mv, g_sub)
        in_specs = [pl.BlockSpec(memory_space=pltpu.SMEM), pl.BlockSpec((4, HEAD_DIM), fixed),
                    *common, pl.BlockSpec((1, GW), fixed)]
        acc_w = 2 * GW
    else:
        f_frames, f_meta = extra
        args = (q, kt, v, mk, mv, f_frames, f_meta)
        in_specs = [*common,
                    pl.BlockSpec((1, 1, 2, t), lambda b, g, i: (b, g, 0, 0)),
                    pl.BlockSpec((1, 1, 2, N_META), lambda b, g, i: (b, g, 0, 0))]
        acc_w = GW
    return pl.pallas_call(
        functools.partial(_frames_attn_kernel, kind, tq, lam_init),
        out_shape=jax.ShapeDtypeStruct((nb, t, HALF), F32),
        grid=(nb, GROUPS, t // tq),
        in_specs=in_specs,
        out_specs=qspec,
        scratch_shapes=[pltpu.VMEM((2, tq, tq), F32), pltpu.VMEM((2, tq, tq), F32),
                        pltpu.VMEM((2, tq, acc_w), F32)],
        compiler_params=pltpu.CompilerParams(
            dimension_semantics=("arbitrary", "arbitrary", "arbitrary"),
            vmem_limit_bytes=VMEM_LIMIT),
        name="frames_attn_" + kind,
    )(*args)


def _small_attn_kernel(kind, tq, lc, ln, lam_init, *refs):
    refs = list(refs)
    o_ref = refs.pop()
    if kind == "diff":
        sc_ref, lam_ref = refs[0:2]
        refs = refs[2:]
    q_ref = refs.pop(0)
    if lc:
        kc_ref, vc_ref = refs[0:2]
        refs = refs[2:]
    kn_ref, vn_ref = refs[0:2]
    refs = refs[2:]
    if kind == "diff":
        g_ref, = refs
        lam = _lambda(lam_ref, lam_init)
        g_sub = g_ref[...]
    else:
        if lc:
            fc_ref = refs.pop(0)
        fn_ref, = refs

    lc_main = (lc // LANES) * LANES
    spans = [sp for sp in ((0, lc_main), (lc_main, lc)) if sp[1] > sp[0]]
    qpos = lc + lax.broadcasted_iota(jnp.int32, (tq, 1), 0)

    for g in range(GROUPS):
        cols = slice(g * GW, (g + 1) * GW)
        qs = _split_lanes(q_ref[0, :, cols], 0)
        segs = []
        for a, b in spans:
            kt = kc_ref[0, 0, cols, a:b].astype(BF16)
            if kind == "diff":
                vb = vc_ref[0, 0, a:b, cols].astype(BF16)
                pv = functools.partial(lambda p, vb: _dot(p, vb), vb=vb)
            else:
                vt = vc_ref[0, 0, cols, a:b].astype(BF16)
                pv = functools.partial(lambda p, vt: _dot_nt(p, vt), vt=vt)
            f = fc_ref[0, g, :, a:b] if kind == "fox" else None
            segs.append((a, b - a, f, functools.partial(lambda q, kt: _dot(q, kt), kt=kt), pv))
        kn = kn_ref[0, :, cols]
        vn = vn_ref[0, :, cols]
        f = fn_ref[0, g] if kind == "fox" else None
        segs.append((lc, ln, f, functools.partial(lambda q, kn: _dot_nt(q, kn), kn=kn),
                     functools.partial(lambda p, vn: _dot(p, vn), vn=vn)))

        os_ = []
        for i in range(2):
            ss = []
            for start, length, f, score, _ in segs:
                kpos = start + lax.broadcasted_iota(jnp.int32, (1, length), 1)
                s = score(qs[i])
                if kind == "diff":
                    s = s - sc_ref[g] * jnp.abs(qpos - kpos).astype(F32)
                    mask = ((kpos - N_META) >> CHUNK_SHIFT) <= ((qpos - N_META) >> CHUNK_SHIFT)
                else:
                    s = s - LOG2E * f[i:i + 1]
                    mask = kpos <= qpos
                ss.append(jnp.where(mask, s, NEG))
            m = functools.reduce(jnp.maximum, [jnp.max(s, axis=-1, keepdims=True) for s in ss])
            ps = [jnp.exp2(s - m) for s in ss]
            l = functools.reduce(jnp.add, [jnp.sum(p, axis=-1, keepdims=True) for p in ps])
            acc = functools.reduce(jnp.add, [seg[4](p.astype(BF16)) for p, seg in zip(ps, segs)])
            os_.append(acc / l)
        if kind == "diff":
            o_ref[0, :, cols] = _diff_out(os_[0], os_[1], lam, lam_init, g_sub)
        else:
            lane = lax.broadcasted_iota(jnp.int32, os_[0].shape, 1)
            o_ref[0, :, cols] = jnp.where(lane < HEAD_DIM, os_[0], os_[1])


def _small_attention(kind, lam_init, layer, q, kc, vc, kn, vn, extra):
    nb, tq, _ = q.shape
    lc = 0 if kc is None else kc.shape[-1]
    ln = kn.shape[1]
    per_b = lambda rows: pl.BlockSpec((1, rows, HALF), lambda b: (b, 0, 0))
    cache = lambda a: pl.BlockSpec((1, 1) + a.shape[2:], lambda b: (layer, b, 0, 0))
    args, in_specs = [], []
    if kind == "diff":
        slopes, lam_vecs, g_sub = extra
        args += [slopes, lam_vecs]
        in_specs += [pl.BlockSpec(memory_space=pltpu.SMEM),
                     pl.BlockSpec((4, HEAD_DIM), lambda b: (0, 0))]
    args.append(q)
    in_specs.append(per_b(tq))
    if lc:
        args += [kc, vc]
        in_specs += [cache(kc), cache(vc)]
    args += [kn, vn]
    in_specs += [per_b(ln), per_b(ln)]
    if kind == "diff":
        args.append(g_sub)
        in_specs.append(pl.BlockSpec((1, GW), lambda b: (0, 0)))
    else:
        f_cache, f_new = extra
        if lc:
            args.append(f_cache)
            in_specs.append(pl.BlockSpec((1, GROUPS, 2, lc), lambda b: (b, 0, 0, 0)))
        args.append(f_new)
        in_specs.append(pl.BlockSpec((1, GROUPS, 2, ln), lambda b: (b, 0, 0, 0)))
    return pl.pallas_call(
        functools.partial(_small_attn_kernel, kind, tq, lc, ln, lam_init),
        out_shape=jax.ShapeDtypeStruct((nb, tq, HALF), F32),
        grid=(nb,),
        in_specs=in_specs,
        out_specs=per_b(tq),
        compiler_params=pltpu.CompilerParams(dimension_semantics=("arbitrary",),
                                             vmem_limit_bytes=VMEM_LIMIT),
        name="small_attn_" + kind,
    )(*args)


def _outproj_kernel(final, ad_ref, af_ref, gate_ref, x_ref, w_ref, gf_ref, y_ref):
    gt = gate_ref[...]
    sg = gt / (1.0 + jnp.exp(-gt))
    ud = (ad_ref[...] * sg[:, 0:HALF]).astype(BF16)
    uf = (af_ref[...] * sg[:, HALF:MIX]).astype(BF16)
    y = x_ref[...] + (_dot(ud, w_ref[0:HALF, :]) + _dot(uf, w_ref[HALF:MIX, :]))
    if final:
        y = _rms(y, gf_ref[...], NORM_EPS)
    y_ref[...] = y


def _out_project(final, ad, af, gate, x, w_out, g_final):
    rows = x.shape[0]
    tm = math.gcd(rows, ROW_TILE)
    row = lambda i: (i, 0)
    fixed = lambda i: (0, 0)
    return pl.pallas_call(
        functools.partial(_outproj_kernel, final),
        out_shape=jax.ShapeDtypeStruct((rows, D_MODEL), F32),
        grid=(rows // tm,),
        in_specs=[pl.BlockSpec((tm, HALF), row), pl.BlockSpec((tm, HALF), row),
                  pl.BlockSpec((tm, MIX), row), pl.BlockSpec((tm, D_MODEL), row),
                  pl.BlockSpec((MIX, D_MODEL), fixed), pl.BlockSpec((1, D_MODEL), fixed)],
        out_specs=pl.BlockSpec((tm, D_MODEL), row),
        compiler_params=pltpu.CompilerParams(dimension_semantics=("arbitrary",),
                                             vmem_limit_bytes=VMEM_LIMIT),
        name="outproj",
    )(ad, af, gate, x, w_out, g_final)


def kernel(x_prompt, x_sample, cache_diff_k, cache_diff_v, cache_fox_k, cache_fox_v, cache_fox_logf,
           meta_tokens, w_in, b_forget, norm_g, w_out, lambda_q1, lambda_k1, lambda_q2, lambda_k2,
           subln_g, final_norm_g):
    depth = w_in.shape[0]
    nb_p, seq, _ = x_prompt.shape
    nb_s, seq_s, _ = x_sample.shape
    lc = cache_diff_k.shape[2]

    c = [n * HALF for n in range(7)]
    fl0, g0 = c[6], c[6] + FOX_HEADS
    w_rows = jnp.concatenate([w_in[:, :, :fl0], w_in[:, :, g0:]], axis=-1).astype(BF16)
    w_f = jnp.pad(w_in[:, :, fl0:g0], ((0, 0), (0, 0), (0, LANES - FOX_HEADS))).astype(BF16)
    b_f = jnp.pad(b_forget.astype(F32), ((0, 0), (0, LANES - FOX_HEADS)))
    w_cols = jnp.concatenate([w_in[:, :, c[0]:c[1]], w_in[:, :, c[2]:c[4]], w_in[:, :, c[5]:c[6]],
                              w_in[:, :, g0:]], axis=-1).astype(BF16)
    w_in_t = jnp.swapaxes(w_in, 1, 2)
    w_t = jnp.concatenate([w_in_t[:, c[1]:c[2]], w_in_t[:, c[4]:c[6]]], axis=1).astype(BF16)
    w_ft = jnp.pad(w_in_t[:, fl0:g0], ((0, 0), (0, FOX_HEADS), (0, 0))).astype(BF16)
    b_ft = b_forget.astype(F32)[:, :, None]
    w_o = w_out.astype(BF16)
    slopes = (2.0 ** (-8.0 * jnp.arange(1, DIFF_HEADS + 1, dtype=F32) / DIFF_HEADS)) * LOG2E
    g_final = final_norm_g.astype(F32).reshape(1, D_MODEL)

    ckd_t = jnp.transpose(cache_diff_k, (0, 1, 3, 4, 5, 2)).reshape(depth, nb_s, HALF, lc)
    cvd = cache_diff_v.reshape(depth, nb_s, lc, HALF)
    ckf_t = jnp.transpose(cache_fox_k, (0, 1, 3, 4, 2)).reshape(depth, nb_s, HALF, lc)
    cvf_t = jnp.transpose(cache_fox_v, (0, 1, 3, 4, 2)).reshape(depth, nb_s, HALF, lc)
    clf_t = jnp.swapaxes(cache_fox_logf, 2, 3).astype(F32)

    xf = x_prompt
    xm = jnp.broadcast_to(meta_tokens[None].astype(x_prompt.dtype),
                          (nb_p, N_META, D_MODEL)).reshape(nb_p * N_META, D_MODEL)
    xs = x_sample.reshape(nb_s * seq_s, D_MODEL)

    rows_p = [[] for _ in range(5)]
    rows_s = [[] for _ in range(5)]
    shape3 = lambda a, nb: a.reshape(nb, -1, a.shape[-1])
    for l in range(depth):
        lam_init = 0.8 - 0.6 * math.exp(-0.3 * l)
        final = l == depth - 1
        g_l = norm_g[l].astype(F32).reshape(1, D_MODEL)
        lam_vecs = jnp.stack([lambda_q1[l], lambda_k1[l], lambda_q2[l], lambda_k2[l]]).astype(F32)
        diff_extra = (slopes, lam_vecs, subln_g[l].astype(F32).reshape(1, GW))

        def out_project(ad, af, gate, x):
            return _out_project(final, ad.reshape(-1, HALF), af.reshape(-1, HALF), gate.reshape(-1, MIX),
                                x.reshape(-1, D_MODEL), w_o[l], g_final)

        (qd, kdt16, vd16, qf, kft16, vf16, kdt32, vd32, kft32, vft32, lft, gate) = _project_frames(
            xf, g_l, w_cols[l], w_t[l], w_ft[l], b_ft[l])
        (mqd, mkd32, mkd16, mvd32, mvd16, mqf, mkf32, mkf16, mvf32, mvf16, mlf, mgate) = [
            shape3(a, nb_p) for a in _project_rows(xm, g_l, w_rows[l], w_f[l], b_f[l:l + 1])]
        logf_t = jnp.concatenate([jnp.swapaxes(mlf[:, :, :FOX_HEADS], 1, 2), lft], axis=-1)
        f_all = _cum_logf(logf_t)
        f_meta = f_all[..., :N_META]
        f_frames = f_all[..., N_META:N_META + seq]

        ad = _frames_attention("diff", lam_init, qd, kdt16, vd16, mkd16, mvd16, diff_extra)
        af = _frames_attention("fox", lam_init, qf, kft16, vf16, mkf16, mvf16, (f_frames, f_meta))
        xf = out_project(ad, af, gate, xf).reshape(nb_p, seq, D_MODEL)
        if not final:
            mad = _small_attention("diff", lam_init, l, mqd, None, None, mkd16, mvd16, diff_extra)
            maf = _small_attention("fox", lam_init, l, mqf, None, None, mkf16, mvf16, (None, f_meta))
            xm = out_project(mad, maf, mgate, xm)
        rows_p[0].append(jnp.concatenate([jnp.swapaxes(mkd32, 1, 2), kdt32], axis=-1))
        rows_p[1].append(jnp.concatenate([mvd32, vd32], axis=1))
        rows_p[2].append(jnp.concatenate([jnp.swapaxes(mkf32, 1, 2), kft32], axis=-1))
        rows_p[3].append(jnp.concatenate([jnp.swapaxes(mvf32, 1, 2), vft32], axis=-1))
        rows_p[4].append(logf_t)

        (sqd, skd32, skd16, svd32, svd16, sqf, skf32, skf16, svf32, svf16, slf, sgate) = [
            shape3(a, nb_s) for a in _project_rows(xs, g_l, w_rows[l], w_f[l], b_f[l:l + 1])]
        logf_s = slf[:, :, :FOX_HEADS]
        f_s = _cum_logf(jnp.concatenate([clf_t[l], jnp.swapaxes(logf_s, 1, 2)], axis=-1))
        sad = _small_attention("diff", lam_init, l, sqd, ckd_t, cvd, skd16, svd16, diff_extra)
        saf = _small_attention("fox", lam_init, l, sqf, ckf_t, cvf_t, skf16, svf16,
                               (f_s[..., :lc], f_s[..., lc:lc + seq_s]))
        xs = out_project(sad, saf, sgate, xs)
        for dst, a in zip(rows_s, (skd32, svd32, skf32, svf32, logf_s)):
            dst.append(a)

    y_prompt = xf
    y_sample = xs.reshape(nb_s, seq_s, D_MODEL)
    tp = N_META + seq
    p_diff_k = jnp.transpose(jnp.stack(rows_p[0]).reshape(depth, nb_p, DIFF_HEADS, 2, HEAD_DIM, tp),
                             (0, 1, 5, 2, 3, 4))
    p_diff_v = jnp.stack(rows_p[1]).reshape(depth, nb_p, tp, DIFF_HEADS, 2 * HEAD_DIM)
    p_fox_k = jnp.transpose(jnp.stack(rows_p[2]).reshape(depth, nb_p, FOX_HEADS, HEAD_DIM, tp), (0, 1, 4, 2, 3))
    p_fox_v = jnp.transpose(jnp.stack(rows_p[3]).reshape(depth, nb_p, FOX_HEADS, HEAD_DIM, tp), (0, 1, 4, 2, 3))
    p_fox_logf = jnp.swapaxes(jnp.stack(rows_p[4]), 2, 3)
    s_diff_k = jnp.stack(rows_s[0]).reshape(depth, nb_s, seq_s, DIFF_HEADS, 2, HEAD_DIM)
    s_diff_v = jnp.stack(rows_s[1]).reshape(depth, nb_s, seq_s, DIFF_HEADS, 2 * HEAD_DIM)
    s_fox_k = jnp.stack(rows_s[2]).reshape(depth, nb_s, seq_s, FOX_HEADS, HEAD_DIM)
    s_fox_v = jnp.stack(rows_s[3]).reshape(depth, nb_s, seq_s, FOX_HEADS, HEAD_DIM)
    s_fox_logf = jnp.stack(rows_s[4])
    return (y_prompt, y_sample, p_diff_k, p_diff_v, p_fox_k, p_fox_v, p_fox_logf,
            s_diff_k, s_diff_v, s_fox_k, s_fox_v, s_fox_logf)
```

```python
import functools
import math

import jax
import jax.numpy as jnp
from jax import lax
from jax.experimental import pallas as pl
from jax.experimental.pallas import tpu as pltpu

D_MODEL = 1024
N_META = 16
CHUNK_SHIFT = 6
HEAD_DIM = 64
DIFF_HEADS = 4
FOX_HEADS = 8
GROUPS = 4
GW = 2 * HEAD_DIM
HALF = GROUPS * GW
MIX = 2 * HALF
NORM_EPS = 1e-6
SUBLN_EPS = 1e-5
NEG = -1e30
LOG2E = 1.4426950408889634
QSCALE = HEAD_DIM ** -0.5 * LOG2E
LANES = 128
CUM_CHUNK = 512
ATTN_BLOCK = 1024
ROW_TILE = 256
VMEM_LIMIT = 56 * 1024 * 1024

F32 = jnp.float32
BF16 = jnp.bfloat16


def _dot(a, b):
    return jnp.dot(a, b, preferred_element_type=F32)


def _dot_nt(a, b):
    return lax.dot_general(a, b, (((1,), (1,)), ((), ())), preferred_element_type=F32)


def _rms(x, g, eps):
    return (x * lax.rsqrt(jnp.mean(x * x, axis=-1, keepdims=True) + eps)) * g


def _log_sigmoid(z):
    return jnp.minimum(z, 0.0) - jnp.log(1.0 + jnp.exp(-jnp.abs(z)))


def _proj_rows_kernel(x_ref, g_ref, w_ref, wf_ref, bf_ref,
                      qd_ref, kd32_ref, kd16_ref, vd32_ref, vd16_ref,
                      qf_ref, kf32_ref, kf16_ref, vf32_ref, vf16_ref, lf_ref, gate_ref):
    hb = _rms(x_ref[...], g_ref[...], NORM_EPS).astype(BF16)

    def mm(c):
        return _dot(hb, w_ref[:, c * HALF:(c + 1) * HALF])

    qd_ref[...] = (mm(0) * QSCALE).astype(BF16)
    for c, r32, r16 in ((1, kd32_ref, kd16_ref), (2, vd32_ref, vd16_ref),
                        (4, kf32_ref, kf16_ref), (5, vf32_ref, vf16_ref)):
        z = mm(c)
        r32[...] = z
        r16[...] = z.astype(BF16)
    qf_ref[...] = (mm(3) * QSCALE).astype(BF16)
    gate_ref[:, 0:HALF] = mm(6)
    gate_ref[:, HALF:MIX] = mm(7)
    lf_ref[...] = _log_sigmoid(_dot(hb, wf_ref[...]) + bf_ref[...])


def _project_rows(x, g, w_rows, w_f, b_f):
    rows = x.shape[0]
    tm = math.gcd(rows, ROW_TILE)
    row = lambda i: (i, 0)
    fixed = lambda i: (0, 0)
    half16 = jax.ShapeDtypeStruct((rows, HALF), BF16)
    half32 = jax.ShapeDtypeStruct((rows, HALF), F32)
    out_shape = (half16, half32, half16, half32, half16,
                 half16, half32, half16, half32, half16,
                 jax.ShapeDtypeStruct((rows, LANES), F32),
                 jax.ShapeDtypeStruct((rows, MIX), F32))
    return pl.pallas_call(
        _proj_rows_kernel,
        out_shape=out_shape,
        grid=(rows // tm,),
        in_specs=[pl.BlockSpec((tm, D_MODEL), row),
                  pl.BlockSpec((1, D_MODEL), fixed),
                  pl.BlockSpec((D_MODEL, 8 * HALF), fixed),
                  pl.BlockSpec((D_MODEL, LANES), fixed),
                  pl.BlockSpec((1, LANES), fixed)],
        out_specs=tuple(pl.BlockSpec((tm, s.shape[1]), row) for s in out_shape),
        compiler_params=pltpu.CompilerParams(dimension_semantics=("arbitrary",),
                                             vmem_limit_bytes=VMEM_LIMIT),
        name="proj_rows",
    )(x, g, w_rows, w_f, b_f)


def _proj_frames_kernel(nt, x_ref, g_ref, w_ref, wt_ref, wft_ref, bft_ref,
                        mkd_ref, mkf_ref, mvf_ref, mlf_ref, skd_in, skf_in, svf_in, slf_in,
                        qd_ref, kdt16_ref, vd16_ref, qf_ref, kft16_ref, vf16_ref, vd32_ref, lft_ref, gate_ref,
                        skd_ref, skf_ref, svf_ref, slf_ref, ckd_ref, ckf_ref, cvf_ref, clf_ref):
    del skd_in, skf_in, svf_in, slf_in
    i = pl.program_id(1)
    shifted = ((ckd_ref, skd_ref, mkd_ref), (ckf_ref, skf_ref, mkf_ref), (cvf_ref, svf_ref, mvf_ref),
               (clf_ref, slf_ref, mlf_ref))

    @pl.when(i == 0)
    def _():
        for carry_ref, _, meta_ref in shifted:
            carry_ref[:, 0:N_META] = meta_ref[0]

    def emit(z, carry_ref, out_ref):
        lane = lax.broadcasted_iota(jnp.int32, z.shape, 1)
        zr = pltpu.roll(z, N_META, 1)
        out_ref[0, 0] = jnp.where(lane < N_META, carry_ref[...], zr)
        carry_ref[...] = zr

    @pl.when(i < nt)
    def _():
        hb = _rms(x_ref[0], g_ref[...], NORM_EPS).astype(BF16)

        def mm(c):
            return _dot(hb, w_ref[:, c * HALF:(c + 1) * HALF])

        def mm_t(c):
            return _dot_nt(wt_ref[c * HALF:(c + 1) * HALF, :], hb)

        qd_ref[0] = (mm(0) * QSCALE).astype(BF16)
        z = mm(1)
        vd32_ref[0] = z
        vd16_ref[0] = z.astype(BF16)
        qf_ref[0] = (mm(2) * QSCALE).astype(BF16)
        vf16_ref[0] = mm(3).astype(BF16)
        gate_ref[0, :, 0:HALF] = mm(4)
        gate_ref[0, :, HALF:MIX] = mm(5)
        z = mm_t(0)
        kdt16_ref[0] = z.astype(BF16)
        emit(z, ckd_ref, skd_ref)
        z = mm_t(1)
        kft16_ref[0] = z.astype(BF16)
        emit(z, ckf_ref, skf_ref)
        emit(mm_t(2), cvf_ref, svf_ref)
        zf = _dot_nt(wft_ref[...], hb)
        lf = _log_sigmoid(zf[0:FOX_HEADS] + bft_ref[...])
        lft_ref[0] = lf
        emit(lf, clf_ref, slf_ref)

    @pl.when(i == nt)
    def _():
        for carry_ref, out_ref, _ in shifted:
            out_ref[0, 0] = carry_ref[...]


def _project_frames(layer, x, g, w_cols, w_t, w_ft, b_ft, metas, stacks):
    nb, t, _ = x.shape
    tm = ROW_TILE
    nt = t // tm
    last = nt - 1
    rows = lambda b, i: (b, jnp.minimum(i, last), 0)
    cols = lambda b, i: (b, 0, jnp.minimum(i, last))
    fixed = lambda b, i: (0, 0)
    rm16 = (jax.ShapeDtypeStruct((nb, t, HALF), BF16), pl.BlockSpec((1, tm, HALF), rows))
    rm32 = (jax.ShapeDtypeStruct((nb, t, HALF), F32), pl.BlockSpec((1, tm, HALF), rows))
    fm16 = (jax.ShapeDtypeStruct((nb, HALF, t), BF16), pl.BlockSpec((1, HALF, tm), cols))
    lft = (jax.ShapeDtypeStruct((nb, FOX_HEADS, t), F32), pl.BlockSpec((1, FOX_HEADS, tm), cols))
    gate = (jax.ShapeDtypeStruct((nb, t, MIX), F32), pl.BlockSpec((1, tm, MIX), rows))
    stacked = [(jax.ShapeDtypeStruct(s.shape, F32),
                pl.BlockSpec((1, 1, s.shape[2], tm), lambda b, i: (layer, b, 0, i))) for s in stacks]
    outs = (rm16, fm16, rm16, rm16, fm16, rm16, rm32, lft, gate, *stacked)
    first_stack_in = 6 + len(metas)
    first_stack_out = len(outs) - len(stacks)
    return pl.pallas_call(
        functools.partial(_proj_frames_kernel, nt),
        out_shape=tuple(o[0] for o in outs),
        grid=(nb, nt + 1),
        in_specs=[pl.BlockSpec((1, tm, D_MODEL), rows),
                  pl.BlockSpec((1, D_MODEL), fixed),
                  pl.BlockSpec((D_MODEL, 6 * HALF), fixed),
                  pl.BlockSpec((3 * HALF, D_MODEL), fixed),
                  pl.BlockSpec((2 * FOX_HEADS, D_MODEL), fixed),
                  pl.BlockSpec((FOX_HEADS, 1), fixed),
                  *[pl.BlockSpec((1, m.shape[1], N_META), lambda b, i: (b, 0, 0)) for m in metas],
                  *[pl.BlockSpec(memory_space=pl.ANY) for _ in stacks]],
        out_specs=tuple(o[1] for o in outs),
        scratch_shapes=[pltpu.VMEM((s.shape[2], tm), F32) for s in stacks],
        input_output_aliases={first_stack_in + k: first_stack_out + k for k in range(len(stacks))},
        compiler_params=pltpu.CompilerParams(dimension_semantics=("arbitrary", "arbitrary"),
                                             vmem_limit_bytes=VMEM_LIMIT),
        name="proj_frames",
    )(x, g, w_cols, w_t, w_ft, b_ft, *metas, *stacks)


def _cumsum_kernel(x_ref, o_ref):
    rows, cols = x_ref.shape
    r = lax.broadcasted_iota(jnp.int32, (CUM_CHUNK, CUM_CHUNK), 0)
    c = lax.broadcasted_iota(jnp.int32, (CUM_CHUNK, CUM_CHUNK), 1)
    tri = (r <= c).astype(BF16)
    carry = jnp.zeros((rows, 1), F32)
    for n in range(cols // CUM_CHUNK):
        x = x_ref[:, n * CUM_CHUNK:(n + 1) * CUM_CHUNK]
        hi = x.astype(BF16)
        r1 = x - hi.astype(F32)
        mid = r1.astype(BF16)
        lo = (r1 - mid.astype(F32)).astype(BF16)
        y = (_dot(hi, tri) + _dot(mid, tri)) + _dot(lo, tri) + carry
        o_ref[:, n * CUM_CHUNK:(n + 1) * CUM_CHUNK] = y
        carry = y[:, CUM_CHUNK - 1:CUM_CHUNK]


def _cum_logf(logf_t):
    nb, h, t = logf_t.shape
    tpad = -(-t // CUM_CHUNK) * CUM_CHUNK
    x = jnp.pad(logf_t.reshape(nb * h, t), ((0, 0), (0, tpad - t)))
    f = pl.pallas_call(
        _cumsum_kernel,
        out_shape=jax.ShapeDtypeStruct((nb * h, tpad), F32),
        compiler_params=pltpu.CompilerParams(vmem_limit_bytes=VMEM_LIMIT),
        name="cumsum",
    )(x)
    return f.reshape(nb, GROUPS, 2, tpad)


def _lambda(lam_ref, lam_init):
    lv = lam_ref[...]
    a = jnp.sum(lv[0:1] * lv[1:2], axis=-1, keepdims=True)
    b = jnp.sum(lv[2:3] * lv[3:4], axis=-1, keepdims=True)
    return jnp.exp(a) - jnp.exp(b) + lam_init


def _split_lanes(x, fill):
    lane = lax.broadcasted_iota(jnp.int32, x.shape, 1)
    f = jnp.full_like(x, fill)
    return jnp.where(lane < HEAD_DIM, x, f), jnp.where(lane >= HEAD_DIM, x, f)


def _diff_out(o1, o2, lam, lam_init, g_sub):
    return _rms(o1 - lam * o2, g_sub, SUBLN_EPS) * (1.0 - lam_init)


def _frames_attn_kernel(kind, tq, lam_init, *refs):
    if kind == "diff":
        (sc_ref, lam_ref, q_ref, kt_ref, v_ref, mk_ref, mv_ref, g_ref, o_ref,
         sa_ref, sb_ref, acc_ref, diag_ref) = refs
    else:
        q_ref, kt_ref, v_ref, mk_ref, mv_ref, fk_ref, fm_ref, o_ref, sa_ref, sb_ref, acc_ref, diag_ref = refs
    gidx = pl.program_id(1)
    qi = pl.program_id(2)
    qs = _split_lanes(q_ref[0], 0)
    s_refs = (sa_ref, sb_ref)
    if kind == "diff":
        slope = sc_ref[gidx]

    @pl.when(qi == 0)
    def _():
        qr = lax.broadcasted_iota(jnp.int32, (tq, tq), 0)
        kr = lax.broadcasted_iota(jnp.int32, (tq, tq), 1)
        if kind == "diff":
            tile = slope * (2 * jnp.minimum(qr, kr) - kr).astype(F32)
            mask = (kr >> CHUNK_SHIFT) <= (qr >> CHUNK_SHIFT)
        else:
            tile = jnp.zeros((tq, tq), F32)
            mask = kr <= qr
        diag_ref[...] = jnp.where(mask, tile, NEG)

    def values(v):
        if kind == "diff":
            ve = jnp.concatenate([v, jnp.ones_like(v)], axis=1)
            return ve, ve
        return _split_lanes(v, 1)

    def produce(n, slot, diag):
        ks = pl.multiple_of(n * tq, tq)
        ktb = kt_ref[0, :, pl.ds(ks, tq)]
        if kind == "diff":
            if diag:
                bias = [diag_ref[...]] * 2
            else:
                kpos = lax.broadcasted_iota(jnp.int32, (1, tq), 1) + (n - qi) * tq
                bias = [slope * kpos.astype(F32)] * 2
        else:
            fk = fk_ref[0, 0, :, pl.ds(ks, tq)]
            bias = [-LOG2E * fk[0:1], -LOG2E * fk[1:2]]
        bms = []
        for i in range(2):
            s = _dot(qs[i], ktb) + bias[i]
            if diag and kind == "fox":
                s = s + diag_ref[...]
            s_refs[slot][i] = s
            bms.append(jnp.max(s, axis=-1, keepdims=True))
        return tuple(bms)

    def consume(n, slot, ms, bms):
        ks = pl.multiple_of(n * tq, tq)
        vs = values(v_ref[0, pl.ds(ks, tq), :])
        out = []
        for i in range(2):
            m_new = jnp.maximum(ms[i], bms[i])
            alpha = jnp.exp2(ms[i] - m_new)
            p = jnp.exp2(s_refs[slot][i] - m_new)
            acc_ref[i] = alpha * acc_ref[i] + _dot(p.astype(BF16), vs[i])
            out.append(m_new)
        return tuple(out)

    mv = mv_ref[0]
    if kind == "diff":
        kpos = lax.broadcasted_iota(jnp.int32, (1, N_META), 1) - (N_META + qi * tq)
        bias_m = [slope * kpos.astype(F32)] * 2
    else:
        fm = fm_ref[0, 0]
        bias_m = [-LOG2E * fm[0:1], -LOG2E * fm[1:2]]
    s = _dot_nt(jnp.concatenate(qs, axis=0), mk_ref[0])
    ms, ps = [], []
    for i in range(2):
        si = s[i * tq:(i + 1) * tq] + bias_m[i]
        m = jnp.max(si, axis=-1, keepdims=True)
        ps.append(jnp.exp2(si - m).astype(BF16))
        ms.append(m)
    r = _dot(jnp.concatenate(ps, axis=0), jnp.concatenate([mv, jnp.ones_like(mv)], axis=1))
    if kind == "diff":
        acc_ref[0] = r[:tq]
        acc_ref[1] = r[tq:]
    else:
        lane = lax.broadcasted_iota(jnp.int32, (tq, GW), 1)
        acc_ref[0] = jnp.where(lane < HEAD_DIM, r[:tq, :GW], r[:tq, GW:])
        acc_ref[1] = jnp.where(lane >= HEAD_DIM, r[tq:, :GW], r[tq:, GW:])
    ms = tuple(ms)

    bm0 = produce(qi, 0, True)

    def pair(t, c):
        ms, bm0 = c[:2], c[2:]
        bm1 = produce(2 * t, 1, False)
        ms = consume(jnp.where(t == 0, qi, 2 * t - 1), 0, ms, bm0)
        bm0 = produce(2 * t + 1, 0, False)
        ms = consume(2 * t, 1, ms, bm1)
        return ms + bm0

    c = lax.fori_loop(0, qi // 2, pair, ms + bm0)
    ms, bm0 = c[:2], c[2:]
    pending = jnp.where(qi < 2, qi, 2 * (qi // 2) - 1)

    def odd(ms, bm0):
        bm1 = produce(qi - 1, 1, False)
        ms = consume(pending, 0, ms, bm0)
        return consume(qi - 1, 1, ms, bm1)

    def even(ms, bm0):
        return consume(pending, 0, ms, bm0)

    lax.cond(qi % 2 == 1, odd, even, ms, bm0)

    a = acc_ref[0]
    b = acc_ref[1]
    if kind == "diff":
        o_ref[0] = _diff_out(a[:, :GW] / a[:, GW:], b[:, :GW] / b[:, GW:],
                             _lambda(lam_ref, lam_init), lam_init, g_ref[...])
    else:
        lane = lax.broadcasted_iota(jnp.int32, a.shape, 1)
        o_ref[0] = jnp.where(lane < HEAD_DIM, a / pltpu.roll(a, HEAD_DIM, 1), b / pltpu.roll(b, HEAD_DIM, 1))


def _frames_attention(kind, lam_init, q, kt, v, mk, mv, extra):
    nb, t, _ = q.shape
    tq = ATTN_BLOCK
    idx = lambda b, g, i: (b, i, g)
    rows_g = lambda b, g, i: (b, 0, g)
    feat_g = lambda b, g, i: (b, g, 0)
    fixed = lambda b, g, i: (0, 0)
    qspec = pl.BlockSpec((1, tq, GW), idx)
    common = [qspec, pl.BlockSpec((1, GW, t), feat_g), pl.BlockSpec((1, t, GW), rows_g),
              pl.BlockSpec((1, N_META, GW), rows_g), pl.BlockSpec((1, N_META, GW), rows_g)]
    if kind == "diff":
        slopes, lam_vecs, g_sub = extra
        args = (slopes, lam_vecs, q, kt, v, mk, mv, g_sub)
        in_specs = [pl.BlockSpec(memory_space=pltpu.SMEM), pl.BlockSpec((4, HEAD_DIM), fixed),
                    *common, pl.BlockSpec((1, GW), fixed)]
        acc_w = 2 * GW
    else:
        f_frames, f_meta = extra
        args = (q, kt, v, mk, mv, f_frames, f_meta)
        in_specs = [*common,
                    pl.BlockSpec((1, 1, 2, t), lambda b, g, i: (b, g, 0, 0)),
                    pl.BlockSpec((1, 1, 2, N_META), lambda b, g, i: (b, g, 0, 0))]
        acc_w = GW
    return pl.pallas_call(
        functools.partial(_frames_attn_kernel, kind, tq, lam_init),
        out_shape=jax.ShapeDtypeStruct((nb, t, HALF), F32),
        grid=(nb, GROUPS, t // tq),
        in_specs=in_specs,
        out_specs=qspec,
        scratch_shapes=[pltpu.VMEM((2, tq, tq), F32), pltpu.VMEM((2, tq, tq), F32),
                        pltpu.VMEM((2, tq, acc_w), F32), pltpu.VMEM((tq, tq), F32)],
        compiler_params=pltpu.CompilerParams(
            dimension_semantics=("arbitrary", "arbitrary", "arbitrary"),
            vmem_limit_bytes=VMEM_LIMIT),
        name="frames_attn_" + kind,
    )(*args)


def _small_attn_kernel(kind, tq, lc, ln, lam_init, *refs):
    refs = list(refs)
    o_ref = refs.pop()
    if kind == "diff":
        sc_ref, lam_ref = refs[0:2]
        refs = refs[2:]
    q_ref = refs.pop(0)
    if lc:
        kc_ref, vc_ref = refs[0:2]
        refs = refs[2:]
    kn_ref, vn_ref = refs[0:2]
    refs = refs[2:]
    if kind == "diff":
        g_ref, = refs
        lam = _lambda(lam_ref, lam_init)
        g_sub = g_ref[...]
    else:
        if lc:
            fc_ref = refs.pop(0)
        fn_ref, = refs

    lc_main = (lc // LANES) * LANES
    spans = [sp for sp in ((0, lc_main), (lc_main, lc)) if sp[1] > sp[0]]
    qpos = lc + lax.broadcasted_iota(jnp.int32, (tq, 1), 0)

    for g in range(GROUPS):
        cols = slice(g * GW, (g + 1) * GW)
        qs = _split_lanes(q_ref[0, :, cols], 0)
        segs = []
        for a, b in spans:
            kt = kc_ref[0, 0, cols, a:b].astype(BF16)
            if kind == "diff":
                vb = vc_ref[0, 0, pl.ds(a * DIFF_HEADS + g, b - a, stride=DIFF_HEADS), :].astype(BF16)
                pv = functools.partial(lambda p, vb: _dot(p, vb), vb=vb)
            else:
                vt = vc_ref[0, 0, cols, a:b].astype(BF16)
                pv = functools.partial(lambda p, vt: _dot_nt(p, vt), vt=vt)
            f = fc_ref[0, g, :, a:b] if kind == "fox" else None
            segs.append((a, b - a, f, functools.partial(lambda q, kt: _dot(q, kt), kt=kt), pv))
        kn = kn_ref[0, :, cols]
        vn = vn_ref[0, :, cols]
        f = fn_ref[0, g] if kind == "fox" else None
        segs.append((lc, ln, f, functools.partial(lambda q, kn: _dot_nt(q, kn), kn=kn),
                     functools.partial(lambda p, vn: _dot(p, vn), vn=vn)))

        os_ = []
        for i in range(2):
            ss = []
            for start, length, f, score, _ in segs:
                kpos = start + lax.broadcasted_iota(jnp.int32, (1, length), 1)
                s = score(qs[i])
                if kind == "diff":
                    s = s - sc_ref[g] * jnp.abs(qpos - kpos).astype(F32)
                    mask = ((kpos - N_META) >> CHUNK_SHIFT) <= ((qpos - N_META) >> CHUNK_SHIFT)
                else:
                    s = s - LOG2E * f[i:i + 1]
                    mask = kpos <= qpos
                ss.append(jnp.where(mask, s, NEG))
            m = functools.reduce(jnp.maximum, [jnp.max(s, axis=-1, keepdims=True) for s in ss])
            ps = [jnp.exp2(s - m) for s in ss]
            l = functools.reduce(jnp.add, [jnp.sum(p, axis=-1, keepdims=True) for p in ps])
            acc = functools.reduce(jnp.add, [seg[4](p.astype(BF16)) for p, seg in zip(ps, segs)])
            os_.append(acc / l)
        if kind == "diff":
            o_ref[0, :, cols] = _diff_out(os_[0], os_[1], lam, lam_init, g_sub)
        else:
            lane = lax.broadcasted_iota(jnp.int32, os_[0].shape, 1)
            o_ref[0, :, cols] = jnp.where(lane < HEAD_DIM, os_[0], os_[1])


def _small_attention(kind, lam_init, layer, q, kc, vc, kn, vn, extra):
    nb, tq, _ = q.shape
    lc = 0 if kc is None else kc.shape[-1]
    ln = kn.shape[1]
    per_b = lambda rows: pl.BlockSpec((1, rows, HALF), lambda b: (b, 0, 0))
    cache = lambda a: pl.BlockSpec((1, 1) + a.shape[2:], lambda b: (layer, b, 0, 0))
    args, in_specs = [], []
    if kind == "diff":
        slopes, lam_vecs, g_sub = extra
        args += [slopes, lam_vecs]
        in_specs += [pl.BlockSpec(memory_space=pltpu.SMEM),
                     pl.BlockSpec((4, HEAD_DIM), lambda b: (0, 0))]
    args.append(q)
    in_specs.append(per_b(tq))
    if lc:
        args += [kc, vc]
        in_specs += [cache(kc), cache(vc)]
    args += [kn, vn]
    in_specs += [per_b(ln), per_b(ln)]
    if kind == "diff":
        args.append(g_sub)
        in_specs.append(pl.BlockSpec((1, GW), lambda b: (0, 0)))
    else:
        f_cache, f_new = extra
        if lc:
            args.append(f_cache)
            in_specs.append(pl.BlockSpec((1, GROUPS, 2, lc), lambda b: (b, 0, 0, 0)))
        args.append(f_new)
        in_specs.append(pl.BlockSpec((1, GROUPS, 2, ln), lambda b: (b, 0, 0, 0)))
    return pl.pallas_call(
        functools.partial(_small_attn_kernel, kind, tq, lc, ln, lam_init),
        out_shape=jax.ShapeDtypeStruct((nb, tq, HALF), F32),
        grid=(nb,),
        in_specs=in_specs,
        out_specs=per_b(tq),
        compiler_params=pltpu.CompilerParams(dimension_semantics=("arbitrary",),
                                             vmem_limit_bytes=VMEM_LIMIT),
        name="small_attn_" + kind,
    )(*args)


def _outproj_kernel(final, ad_ref, af_ref, gate_ref, x_ref, w_ref, gf_ref, y_ref):
    gt = gate_ref[...]
    sg = gt / (1.0 + jnp.exp(-gt))
    ud = (ad_ref[...] * sg[:, 0:HALF]).astype(BF16)
    uf = (af_ref[...] * sg[:, HALF:MIX]).astype(BF16)
    y = x_ref[...] + (_dot(ud, w_ref[0:HALF, :]) + _dot(uf, w_ref[HALF:MIX, :]))
    if final:
        y = _rms(y, gf_ref[...], NORM_EPS)
    y_ref[...] = y


def _out_project(final, ad, af, gate, x, w_out, g_final):
    rows = x.shape[0]
    tm = math.gcd(rows, ROW_TILE)
    row = lambda i: (i, 0)
    fixed = lambda i: (0, 0)
    return pl.pallas_call(
        functools.partial(_outproj_kernel, final),
        out_shape=jax.ShapeDtypeStruct((rows, D_MODEL), F32),
        grid=(rows // tm,),
        in_specs=[pl.BlockSpec((tm, HALF), row), pl.BlockSpec((tm, HALF), row),
                  pl.BlockSpec((tm, MIX), row), pl.BlockSpec((tm, D_MODEL), row),
                  pl.BlockSpec((MIX, D_MODEL), fixed), pl.BlockSpec((1, D_MODEL), fixed)],
        out_specs=pl.BlockSpec((tm, D_MODEL), row),
        compiler_params=pltpu.CompilerParams(dimension_semantics=("arbitrary",),
                                             vmem_limit_bytes=VMEM_LIMIT),
        name="outproj",
    )(ad, af, gate, x, w_out, g_final)


def kernel(x_prompt, x_sample, cache_diff_k, cache_diff_v, cache_fox_k, cache_fox_v, cache_fox_logf,
           meta_tokens, w_in, b_forget, norm_g, w_out, lambda_q1, lambda_k1, lambda_q2, lambda_k2,
           subln_g, final_norm_g):
    depth = w_in.shape[0]
    nb_p, seq, _ = x_prompt.shape
    nb_s, seq_s, _ = x_sample.shape
    lc = cache_diff_k.shape[2]

    c = [n * HALF for n in range(7)]
    fl0, g0 = c[6], c[6] + FOX_HEADS
    w_rows = jnp.concatenate([w_in[:, :, :fl0], w_in[:, :, g0:]], axis=-1).astype(BF16)
    w_f = jnp.pad(w_in[:, :, fl0:g0], ((0, 0), (0, 0), (0, LANES - FOX_HEADS))).astype(BF16)
    b_f = jnp.pad(b_forget.astype(F32), ((0, 0), (0, LANES - FOX_HEADS)))
    w_cols = jnp.concatenate([w_in[:, :, c[0]:c[1]], w_in[:, :, c[2]:c[4]], w_in[:, :, c[5]:c[6]],
                              w_in[:, :, g0:]], axis=-1).astype(BF16)
    w_in_t = jnp.swapaxes(w_in, 1, 2)
    w_t = jnp.concatenate([w_in_t[:, c[1]:c[2]], w_in_t[:, c[4]:c[6]]], axis=1).astype(BF16)
    w_ft = jnp.pad(w_in_t[:, fl0:g0], ((0, 0), (0, FOX_HEADS), (0, 0))).astype(BF16)
    b_ft = b_forget.astype(F32)[:, :, None]
    w_o = w_out.astype(BF16)
    slopes = (2.0 ** (-8.0 * jnp.arange(1, DIFF_HEADS + 1, dtype=F32) / DIFF_HEADS)) * LOG2E
    g_final = final_norm_g.astype(F32).reshape(1, D_MODEL)

    ckd_t = jnp.transpose(cache_diff_k, (0, 1, 3, 4, 5, 2)).reshape(depth, nb_s, HALF, lc)
    cvd = cache_diff_v.reshape(depth, nb_s, lc * DIFF_HEADS, GW)
    ckf_t = jnp.transpose(cache_fox_k, (0, 1, 3, 4, 2)).reshape(depth, nb_s, HALF, lc)
    cvf_t = jnp.transpose(cache_fox_v, (0, 1, 3, 4, 2)).reshape(depth, nb_s, HALF, lc)
    clf_t = jnp.swapaxes(cache_fox_logf, 2, 3).astype(F32)

    xf = x_prompt
    xm = jnp.broadcast_to(meta_tokens[None].astype(x_prompt.dtype),
                          (nb_p, N_META, D_MODEL)).reshape(nb_p * N_META, D_MODEL)
    xs = x_sample.reshape(nb_s * seq_s, D_MODEL)

    tp = N_META + seq
    stacks = tuple(jnp.zeros((depth, nb_p, r, tp), F32) for r in (HALF, HALF, HALF, FOX_HEADS))
    rows_pv = []
    rows_s = [[] for _ in range(5)]
    shape3 = lambda a, nb: a.reshape(nb, -1, a.shape[-1])
    for l in range(depth):
        lam_init = 0.8 - 0.6 * math.exp(-0.3 * l)
        final = l == depth - 1
        g_l = norm_g[l].astype(F32).reshape(1, D_MODEL)
        lam_vecs = jnp.stack([lambda_q1[l], lambda_k1[l], lambda_q2[l], lambda_k2[l]]).astype(F32)
        diff_extra = (slopes, lam_vecs, subln_g[l].astype(F32).reshape(1, GW))

        def out_project(ad, af, gate, x):
            return _out_project(final, ad.reshape(-1, HALF), af.reshape(-1, HALF), gate.reshape(-1, MIX),
                                x.reshape(-1, D_MODEL), w_o[l], g_final)

        (mqd, mkd32, mkd16, mvd32, mvd16, mqf, mkf32, mkf16, mvf32, mvf16, mlf, mgate) = [
            shape3(a, nb_p) for a in _project_rows(xm, g_l, w_rows[l], w_f[l], b_f[l:l + 1])]
        mlf_t = jnp.swapaxes(mlf[:, :, :FOX_HEADS], 1, 2)
        metas = (jnp.swapaxes(mkd32, 1, 2), jnp.swapaxes(mkf32, 1, 2), jnp.swapaxes(mvf32, 1, 2), mlf_t)
        (qd, kdt16, vd16, qf, kft16, vf16, vd32, lft, gate, *stacks) = _project_frames(
            l, xf, g_l, w_cols[l], w_t[l], w_ft[l], b_ft[l], metas, stacks)
        logf_t = jnp.concatenate([mlf_t, lft], axis=-1)
        f_all = _cum_logf(logf_t)
        f_meta = f_all[..., :N_META]
        f_frames = f_all[..., N_META:N_META + seq]

        ad = _frames_attention("diff", lam_init, qd, kdt16, vd16, mkd16, mvd16, diff_extra)
        af = _frames_attention("fox", lam_init, qf, kft16, vf16, mkf16, mvf16, (f_frames, f_meta))
        xf = out_project(ad, af, gate, xf).reshape(nb_p, seq, D_MODEL)
        if not final:
            mad = _small_attention("diff", lam_init, l, mqd, None, None, mkd16, mvd16, diff_extra)
            maf = _small_attention("fox", lam_init, l, mqf, None, None, mkf16, mvf16, (None, f_meta))
            xm = out_project(mad, maf, mgate, xm)
        rows_pv.append(jnp.concatenate([mvd32, vd32], axis=1))

        (sqd, skd32, skd16, svd32, svd16, sqf, skf32, skf16, svf32, svf16, slf, sgate) = [
            shape3(a, nb_s) for a in _project_rows(xs, g_l, w_rows[l], w_f[l], b_f[l:l + 1])]
        logf_s = slf[:, :, :FOX_HEADS]
        f_s = _cum_logf(jnp.concatenate([clf_t[l], jnp.swapaxes(logf_s, 1, 2)], axis=-1))
        sad = _small_attention("diff", lam_init, l, sqd, ckd_t, cvd, skd16, svd16, diff_extra)
        saf = _small_attention("fox", lam_init, l, sqf, ckf_t, cvf_t, skf16, svf16,
                               (f_s[..., :lc], f_s[..., lc:lc + seq_s]))
        xs = out_project(sad, saf, sgate, xs)
        for dst, a in zip(rows_s, (skd32, svd32, skf32, svf32, logf_s)):
            dst.append(a)

    y_prompt = xf
    y_sample = xs.reshape(nb_s, seq_s, D_MODEL)
    skd, skf, svf, slf_stack = stacks
    p_diff_k = jnp.transpose(skd.reshape(depth, nb_p, DIFF_HEADS, 2, HEAD_DIM, tp), (0, 1, 5, 2, 3, 4))
    p_diff_v = jnp.stack(rows_pv).reshape(depth, nb_p, tp, DIFF_HEADS, 2 * HEAD_DIM)
    p_fox_k = jnp.transpose(skf.reshape(depth, nb_p, FOX_HEADS, HEAD_DIM, tp), (0, 1, 4, 2, 3))
    p_fox_v = jnp.transpose(svf.reshape(depth, nb_p, FOX_HEADS, HEAD_DIM, tp), (0, 1, 4, 2, 3))
    p_fox_logf = jnp.swapaxes(slf_stack, 2, 3)
    s_diff_k = jnp.stack(rows_s[0]).reshape(depth, nb_s, seq_s, DIFF_HEADS, 2, HEAD_DIM)
    s_diff_v = jnp.stack(rows_s[1]).reshape(depth, nb_s, seq_s, DIFF_HEADS, 2 * HEAD_DIM)
    s_fox_k = jnp.stack(rows_s[2]).reshape(depth, nb_s, seq_s, FOX_HEADS, HEAD_DIM)
    s_fox_v = jnp.stack(rows_s[3]).reshape(depth, nb_s, seq_s, FOX_HEADS, HEAD_DIM)
    s_fox_logf = jnp.stack(rows_s[4])
    return (y_prompt, y_sample, p_diff_k, p_diff_v, p_fox_k, p_fox_v, p_fox_logf,
            s_diff_k, s_diff_v, s_fox_k, s_fox_v, s_fox_logf)
```

```python
import functools
import math

import jax
import jax.numpy as jnp
from jax import lax
from jax.experimental import pallas as pl
from jax.experimental.pallas import tpu as pltpu

D_MODEL = 1024
N_META = 16
CHUNK_SHIFT = 6
HEAD_DIM = 64
DIFF_HEADS = 4
FOX_HEADS = 8
GROUPS = 4
GW = 2 * HEAD_DIM
HALF = GROUPS * GW
MIX = 2 * HALF
NORM_EPS = 1e-6
SUBLN_EPS = 1e-5
NEG = -1e30
LOG2E = 1.4426950408889634
QSCALE = HEAD_DIM ** -0.5 * LOG2E
LANES = 128
CUM_CHUNK = 512
ATTN_BLOCK = 1024
ROW_TILE = 256
VMEM_LIMIT = 56 * 1024 * 1024

F32 = jnp.float32
BF16 = jnp.bfloat16


def _dot(a, b):
    return jnp.dot(a, b, preferred_element_type=F32)


def _dot_nt(a, b):
    return lax.dot_general(a, b, (((1,), (1,)), ((), ())), preferred_element_type=F32)


def _rms(x, g, eps):
    return (x * lax.rsqrt(jnp.mean(x * x, axis=-1, keepdims=True) + eps)) * g


def _log_sigmoid(z):
    return jnp.minimum(z, 0.0) - jnp.log(1.0 + jnp.exp(-jnp.abs(z)))


def _proj_rows_kernel(x_ref, g_ref, w_ref, wf_ref, bf_ref,
                      qd_ref, kd32_ref, kd16_ref, vd32_ref, vd16_ref,
                      qf_ref, kf32_ref, kf16_ref, vf32_ref, vf16_ref, lf_ref, gate_ref):
    hb = _rms(x_ref[...], g_ref[...], NORM_EPS).astype(BF16)

    def mm(c):
        return _dot(hb, w_ref[:, c * HALF:(c + 1) * HALF])

    qd_ref[...] = (mm(0) * QSCALE).astype(BF16)
    for c, r32, r16 in ((1, kd32_ref, kd16_ref), (2, vd32_ref, vd16_ref),
                        (4, kf32_ref, kf16_ref), (5, vf32_ref, vf16_ref)):
        z = mm(c)
        r32[...] = z
        r16[...] = z.astype(BF16)
    qf_ref[...] = (mm(3) * QSCALE).astype(BF16)
    gate_ref[:, 0:HALF] = mm(6)
    gate_ref[:, HALF:MIX] = mm(7)
    lf_ref[...] = _log_sigmoid(_dot(hb, wf_ref[...]) + bf_ref[...])


def _project_rows(layer, x, g, w_rows, w_f, b_f):
    rows = x.shape[0]
    tm = math.gcd(rows, ROW_TILE)
    row = lambda i: (i, 0)
    of_layer = lambda i: (layer, 0, 0)
    half16 = jax.ShapeDtypeStruct((rows, HALF), BF16)
    half32 = jax.ShapeDtypeStruct((rows, HALF), F32)
    out_shape = (half16, half32, half16, half32, half16,
                 half16, half32, half16, half32, half16,
                 jax.ShapeDtypeStruct((rows, LANES), F32),
                 jax.ShapeDtypeStruct((rows, MIX), F32))
    return pl.pallas_call(
        _proj_rows_kernel,
        out_shape=out_shape,
        grid=(rows // tm,),
        in_specs=[pl.BlockSpec((tm, D_MODEL), row),
                  pl.BlockSpec((None, 1, D_MODEL), of_layer),
                  pl.BlockSpec((None, D_MODEL, 8 * HALF), of_layer),
                  pl.BlockSpec((None, D_MODEL, LANES), of_layer),
                  pl.BlockSpec((None, 1, LANES), of_layer)],
        out_specs=tuple(pl.BlockSpec((tm, s.shape[1]), row) for s in out_shape),
        compiler_params=pltpu.CompilerParams(dimension_semantics=("arbitrary",),
                                             vmem_limit_bytes=VMEM_LIMIT),
        name="proj_rows",
    )(x, g, w_rows, w_f, b_f)


def _proj_frames_kernel(nt, n_aliased, x_ref, g_ref, w_ref, wt_ref, wft_ref, bft_ref,
                        mkd_ref, mkf_ref, mvf_ref, mlf_ref, mvd_ref, *refs):
    (qd_ref, kdt16_ref, vd16_ref, qf_ref, kft16_ref, vf16_ref, lft_ref, gate_ref,
     skd_ref, skf_ref, svf_ref, slf_ref, svd_ref,
     ckd_ref, ckf_ref, cvf_ref, clf_ref, cvd_ref, stage_ref) = refs[n_aliased:]
    i = pl.program_id(1)
    tm = x_ref.shape[1]
    vrows = DIFF_HEADS * tm
    vtail = DIFF_HEADS * N_META
    shifted = ((ckd_ref, skd_ref, mkd_ref), (ckf_ref, skf_ref, mkf_ref), (cvf_ref, svf_ref, mvf_ref),
               (clf_ref, slf_ref, mlf_ref))

    @pl.when(i == 0)
    def _():
        for carry_ref, _, meta_ref in shifted:
            carry_ref[:, 0:N_META] = meta_ref[0]
        cvd_ref[...] = mvd_ref[0]

    def emit(z, carry_ref, out_ref):
        lane = lax.broadcasted_iota(jnp.int32, z.shape, 1)
        zr = pltpu.roll(z, N_META, 1)
        out_ref[0, 0] = jnp.where(lane < N_META, carry_ref[...], zr)
        carry_ref[...] = zr

    @pl.when(i < nt)
    def _():
        hb = _rms(x_ref[0], g_ref[...], NORM_EPS).astype(BF16)

        def mm(c):
            return _dot(hb, w_ref[:, c * HALF:(c + 1) * HALF])

        def mm_t(c):
            return _dot_nt(wt_ref[c * HALF:(c + 1) * HALF, :], hb)

        qd_ref[0] = (mm(0) * QSCALE).astype(BF16)
        z = mm(1)
        vd16_ref[0] = z.astype(BF16)
        for h in range(DIFF_HEADS):
            stage_ref[pl.ds(h, tm, stride=DIFF_HEADS), :] = z[:, h * GW:(h + 1) * GW]
        svd_ref[0, 0, 0:vtail, :] = cvd_ref[...]
        svd_ref[0, 0, vtail:vrows, :] = stage_ref[0:vrows - vtail, :]
        cvd_ref[...] = stage_ref[vrows - vtail:vrows, :]
        qf_ref[0] = (mm(2) * QSCALE).astype(BF16)
        vf16_ref[0] = mm(3).astype(BF16)
        gate_ref[0, :, 0:HALF] = mm(4)
        gate_ref[0, :, HALF:MIX] = mm(5)
        z = mm_t(0)
        kdt16_ref[0] = z.astype(BF16)
        emit(z, ckd_ref, skd_ref)
        z = mm_t(1)
        kft16_ref[0] = z.astype(BF16)
        emit(z, ckf_ref, skf_ref)
        emit(mm_t(2), cvf_ref, svf_ref)
        zf = _dot_nt(wft_ref[...], hb)
        lf = _log_sigmoid(zf[0:FOX_HEADS] + bft_ref[...])
        lft_ref[0] = lf
        emit(lf, clf_ref, slf_ref)

    @pl.when(i == nt)
    def _():
        for carry_ref, out_ref, _ in shifted:
            out_ref[0, 0] = carry_ref[...]
        svd_ref[0, 0, 0:vtail, :] = cvd_ref[...]


def _project_frames(layer, x, g, w_cols, w_t, w_ft, b_ft, metas, meta_vd, stacks):
    depth = g.shape[0]
    nb, t, _ = x.shape
    tm = ROW_TILE
    nt = t // tm
    last = nt - 1
    tp = N_META + t
    rows = lambda b, i: (b, jnp.minimum(i, last), 0)
    cols = lambda b, i: (b, 0, jnp.minimum(i, last))
    of_layer = lambda b, i: (layer, 0, 0)
    rm16 = (jax.ShapeDtypeStruct((nb, t, HALF), BF16), pl.BlockSpec((1, tm, HALF), rows))
    fm16 = (jax.ShapeDtypeStruct((nb, HALF, t), BF16), pl.BlockSpec((1, HALF, tm), cols))
    lft = (jax.ShapeDtypeStruct((nb, FOX_HEADS, t), F32), pl.BlockSpec((1, FOX_HEADS, tm), cols))
    gate = (jax.ShapeDtypeStruct((nb, t, MIX), F32), pl.BlockSpec((1, tm, MIX), rows))
    stacked = [(jax.ShapeDtypeStruct((depth, nb, m.shape[1], tp), F32),
                pl.BlockSpec((1, 1, m.shape[1], tm), lambda b, i: (layer, b, 0, i))) for m in metas]
    stacked.append((jax.ShapeDtypeStruct((depth, nb, DIFF_HEADS * tp, GW), F32),
                    pl.BlockSpec((1, 1, DIFF_HEADS * tm, GW), lambda b, i: (layer, b, i, 0))))
    outs = (rm16, fm16, rm16, rm16, fm16, rm16, lft, gate, *stacked)
    aliased = () if stacks is None else tuple(stacks)
    first_stack_in = 7 + len(metas)
    first_stack_out = len(outs) - len(stacked)
    return pl.pallas_call(
        functools.partial(_proj_frames_kernel, nt, len(aliased)),
        out_shape=tuple(o[0] for o in outs),
        grid=(nb, nt + 1),
        in_specs=[pl.BlockSpec((1, tm, D_MODEL), rows),
                  pl.BlockSpec((None, 1, D_MODEL), of_layer),
                  pl.BlockSpec((None, D_MODEL, 6 * HALF), of_layer),
                  pl.BlockSpec((None, 3 * HALF, D_MODEL), of_layer),
                  pl.BlockSpec((None, 2 * FOX_HEADS, D_MODEL), of_layer),
                  pl.BlockSpec((None, FOX_HEADS, 1), of_layer),
                  *[pl.BlockSpec((1, m.shape[1], N_META), lambda b, i: (b, 0, 0)) for m in metas],
                  pl.BlockSpec((1, DIFF_HEADS * N_META, GW), lambda b, i: (b, 0, 0)),
                  *[pl.BlockSpec(memory_space=pl.ANY) for _ in aliased]],
        out_specs=tuple(o[1] for o in outs),
        scratch_shapes=[*[pltpu.VMEM((m.shape[1], tm), F32) for m in metas],
                        pltpu.VMEM((DIFF_HEADS * N_META, GW), F32),
                        pltpu.VMEM((DIFF_HEADS * tm, GW), F32)],
        input_output_aliases={first_stack_in + k: first_stack_out + k for k in range(len(aliased))},
        compiler_params=pltpu.CompilerParams(dimension_semantics=("arbitrary", "arbitrary"),
                                             vmem_limit_bytes=VMEM_LIMIT),
        name="proj_frames",
    )(x, g, w_cols, w_t, w_ft, b_ft, *metas, meta_vd, *aliased)


def _cumsum_kernel(x_ref, o_ref):
    rows, cols = x_ref.shape
    r = lax.broadcasted_iota(jnp.int32, (CUM_CHUNK, CUM_CHUNK), 0)
    c = lax.broadcasted_iota(jnp.int32, (CUM_CHUNK, CUM_CHUNK), 1)
    tri = (r <= c).astype(BF16)
    carry = jnp.zeros((rows, 1), F32)
    for n in range(cols // CUM_CHUNK):
        x = x_ref[:, n * CUM_CHUNK:(n + 1) * CUM_CHUNK]
        hi = x.astype(BF16)
        r1 = x - hi.astype(F32)
        mid = r1.astype(BF16)
        lo = (r1 - mid.astype(F32)).astype(BF16)
        y = (_dot(hi, tri) + _dot(mid, tri)) + _dot(lo, tri) + carry
        o_ref[:, n * CUM_CHUNK:(n + 1) * CUM_CHUNK] = y
        carry = y[:, CUM_CHUNK - 1:CUM_CHUNK]


def _cum_logf(logf_t):
    nb, h, t = logf_t.shape
    tpad = -(-t // CUM_CHUNK) * CUM_CHUNK
    x = jnp.pad(logf_t.reshape(nb * h, t), ((0, 0), (0, tpad - t)))
    f = pl.pallas_call(
        _cumsum_kernel,
        out_shape=jax.ShapeDtypeStruct((nb * h, tpad), F32),
        compiler_params=pltpu.CompilerParams(vmem_limit_bytes=VMEM_LIMIT),
        name="cumsum",
    )(x)
    return f.reshape(nb, GROUPS, 2, tpad)


def _lambda(lam_ref, lam_init):
    lv = lam_ref[...]
    a = jnp.sum(lv[0:1] * lv[1:2], axis=-1, keepdims=True)
    b = jnp.sum(lv[2:3] * lv[3:4], axis=-1, keepdims=True)
    return jnp.exp(a) - jnp.exp(b) + lam_init


def _split_lanes(x, fill):
    lane = lax.broadcasted_iota(jnp.int32, x.shape, 1)
    f = jnp.full_like(x, fill)
    return jnp.where(lane < HEAD_DIM, x, f), jnp.where(lane >= HEAD_DIM, x, f)


def _diff_out(o1, o2, lam, lam_init, g_sub):
    return _rms(o1 - lam * o2, g_sub, SUBLN_EPS) * (1.0 - lam_init)


def _frames_attn_kernel(kind, tq, lam_init, *refs):
    if kind == "diff":
        (sc_ref, lam_ref, q_ref, kt_ref, v_ref, mk_ref, mv_ref, g_ref, o_ref,
         sa_ref, sb_ref, acc_ref, diag_ref) = refs
    else:
        q_ref, kt_ref, v_ref, mk_ref, mv_ref, fk_ref, fm_ref, o_ref, sa_ref, sb_ref, acc_ref, diag_ref = refs
    gidx = pl.program_id(1)
    qi = pl.program_id(2)
    qs = _split_lanes(q_ref[0], 0)
    s_refs = (sa_ref, sb_ref)
    if kind == "diff":
        slope = sc_ref[gidx]

    @pl.when(qi == 0)
    def _():
        qr = lax.broadcasted_iota(jnp.int32, (tq, tq), 0)
        kr = lax.broadcasted_iota(jnp.int32, (tq, tq), 1)
        if kind == "diff":
            tile = slope * (2 * jnp.minimum(qr, kr) - kr).astype(F32)
            mask = (kr >> CHUNK_SHIFT) <= (qr >> CHUNK_SHIFT)
        else:
            tile = jnp.zeros((tq, tq), F32)
            mask = kr <= qr
        diag_ref[...] = jnp.where(mask, tile, NEG)

    def values(v):
        if kind == "diff":
            ve = jnp.concatenate([v, jnp.ones_like(v)], axis=1)
            return ve, ve
        return _split_lanes(v, 1)

    def produce(n, slot, diag):
        ks = pl.multiple_of(n * tq, tq)
        ktb = kt_ref[0, :, pl.ds(ks, tq)]
        if kind == "diff":
            if diag:
                bias = [diag_ref[...]] * 2
            else:
                kpos = lax.broadcasted_iota(jnp.int32, (1, tq), 1) + (n - qi) * tq
                bias = [slope * kpos.astype(F32)] * 2
        else:
            fk = fk_ref[0, 0, :, pl.ds(ks, tq)]
            bias = [-LOG2E * fk[0:1], -LOG2E * fk[1:2]]
        bms = []
        for i in range(2):
            s = _dot(qs[i], ktb) + bias[i]
            if diag and kind == "fox":
                s = s + diag_ref[...]
            s_refs[slot][i] = s
            bms.append(jnp.max(s, axis=-1, keepdims=True))
        return tuple(bms)

    def consume(n, slot, ms, bms):
        ks = pl.multiple_of(n * tq, tq)
        vs = values(v_ref[0, pl.ds(ks, tq), :])
        out = []
        for i in range(2):
            m_new = jnp.maximum(ms[i], bms[i])
            alpha = jnp.exp2(ms[i] - m_new)
            p = jnp.exp2(s_refs[slot][i] - m_new)
            acc_ref[i] = alpha * acc_ref[i] + _dot(p.astype(BF16), vs[i])
            out.append(m_new)
        return tuple(out)

    mv = mv_ref[0]
    if kind == "diff":
        kpos = lax.broadcasted_iota(jnp.int32, (1, N_META), 1) - (N_META + qi * tq)
        bias_m = [slope * kpos.astype(F32)] * 2
    else:
        fm = fm_ref[0, 0]
        bias_m = [-LOG2E * fm[0:1], -LOG2E * fm[1:2]]
    s = _dot_nt(jnp.concatenate(qs, axis=0), mk_ref[0])
    ms, ps = [], []
    for i in range(2):
        si = s[i * tq:(i + 1) * tq] + bias_m[i]
        m = jnp.max(si, axis=-1, keepdims=True)
        ps.append(jnp.exp2(si - m).astype(BF16))
        ms.append(m)
    r = _dot(jnp.concatenate(ps, axis=0), jnp.concatenate([mv, jnp.ones_like(mv)], axis=1))
    if kind == "diff":
        acc_ref[0] = r[:tq]
        acc_ref[1] = r[tq:]
    else:
        lane = lax.broadcasted_iota(jnp.int32, (tq, GW), 1)
        acc_ref[0] = jnp.where(lane < HEAD_DIM, r[:tq, :GW], r[:tq, GW:])
        acc_ref[1] = jnp.where(lane >= HEAD_DIM, r[tq:, :GW], r[tq:, GW:])
    ms = tuple(ms)

    bm0 = produce(qi, 0, True)

    def pair(t, c):
        ms, bm0 = c[:2], c[2:]
        bm1 = produce(2 * t, 1, False)
        ms = consume(jnp.where(t == 0, qi, 2 * t - 1), 0, ms, bm0)
        bm0 = produce(2 * t + 1, 0, False)
        ms = consume(2 * t, 1, ms, bm1)
        return ms + bm0

    c = lax.fori_loop(0, qi // 2, pair, ms + bm0)
    ms, bm0 = c[:2], c[2:]
    pending = jnp.where(qi < 2, qi, 2 * (qi // 2) - 1)

    def odd(ms, bm0):
        bm1 = produce(qi - 1, 1, False)
        ms = consume(pending, 0, ms, bm0)
        return consume(qi - 1, 1, ms, bm1)

    def even(ms, bm0):
        return consume(pending, 0, ms, bm0)

    lax.cond(qi % 2 == 1, odd, even, ms, bm0)

    a = acc_ref[0]
    b = acc_ref[1]
    if kind == "diff":
        o_ref[0] = _diff_out(a[:, :GW] / a[:, GW:], b[:, :GW] / b[:, GW:],
                             _lambda(lam_ref, lam_init), lam_init, g_ref[...])
    else:
        lane = lax.broadcasted_iota(jnp.int32, a.shape, 1)
        o_ref[0] = jnp.where(lane < HEAD_DIM, a / pltpu.roll(a, HEAD_DIM, 1), b / pltpu.roll(b, HEAD_DIM, 1))


def _frames_attention(kind, lam_init, q, kt, v, mk, mv, extra):
    nb, t, _ = q.shape
    tq = ATTN_BLOCK
    idx = lambda b, g, i: (b, i, g)
    rows_g = lambda b, g, i: (b, 0, g)
    feat_g = lambda b, g, i: (b, g, 0)
    fixed = lambda b, g, i: (0, 0)
    qspec = pl.BlockSpec((1, tq, GW), idx)
    common = [qspec, pl.BlockSpec((1, GW, t), feat_g), pl.BlockSpec((1, t, GW), rows_g),
              pl.BlockSpec((1, N_META, GW), rows_g), pl.BlockSpec((1, N_META, GW), rows_g)]
    if kind == "diff":
        slopes, lam_vecs, g_sub = extra
        args = (slopes, lam_vecs, q, kt, v, mk, mv, g_sub)
        in_specs = [pl.BlockSpec(memory_space=pltpu.SMEM), pl.BlockSpec((4, HEAD_DIM), fixed),
                    *common, pl.BlockSpec((1, GW), fixed)]
        acc_w = 2 * GW
    else:
        f_frames, f_meta = extra
        args = (q, kt, v, mk, mv, f_frames, f_meta)
        in_specs = [*common,
                    pl.BlockSpec((1, 1, 2, t), lambda b, g, i: (b, g, 0, 0)),
                    pl.BlockSpec((1, 1, 2, N_META), lambda b, g, i: (b, g, 0, 0))]
        acc_w = GW
    return pl.pallas_call(
        functools.partial(_frames_attn_kernel, kind, tq, lam_init),
        out_shape=jax.ShapeDtypeStruct((nb, t, HALF), F32),
        grid=(nb, GROUPS, t // tq),
        in_specs=in_specs,
        out_specs=qspec,
        scratch_shapes=[pltpu.VMEM((2, tq, tq), F32), pltpu.VMEM((2, tq, tq), F32),
                        pltpu.VMEM((2, tq, acc_w), F32), pltpu.VMEM((tq, tq), F32)],
        compiler_params=pltpu.CompilerParams(
            dimension_semantics=("arbitrary", "arbitrary", "arbitrary"),
            vmem_limit_bytes=VMEM_LIMIT),
        name="frames_attn_" + kind,
    )(*args)


def _small_attn_kernel(kind, tq, lc, ln, lam_init, *refs):
    refs = list(refs)
    o_ref = refs.pop()
    if kind == "diff":
        sc_ref, lam_ref = refs[0:2]
        refs = refs[2:]
    q_ref = refs.pop(0)
    if lc:
        kc_ref, vc_ref = refs[0:2]
        refs = refs[2:]
    kn_ref, vn_ref = refs[0:2]
    refs = refs[2:]
    if kind == "diff":
        g_ref, = refs
        lam = _lambda(lam_ref, lam_init)
        g_sub = g_ref[...]
    else:
        if lc:
            fc_ref = refs.pop(0)
        fn_ref, = refs

    lc_main = (lc // LANES) * LANES
    spans = [sp for sp in ((0, lc_main), (lc_main, lc)) if sp[1] > sp[0]]
    qpos = lc + lax.broadcasted_iota(jnp.int32, (tq, 1), 0)

    for g in range(GROUPS):
        cols = slice(g * GW, (g + 1) * GW)
        qs = _split_lanes(q_ref[0, :, cols], 0)
        segs = []
        for a, b in spans:
            kt = kc_ref[0, 0, cols, a:b].astype(BF16)
            if kind == "diff":
                vb = vc_ref[0, 0, pl.ds(a * DIFF_HEADS + g, b - a, stride=DIFF_HEADS), :].astype(BF16)
                pv = functools.partial(lambda p, vb: _dot(p, vb), vb=vb)
            else:
                vt = vc_ref[0, 0, cols, a:b].astype(BF16)
                pv = functools.partial(lambda p, vt: _dot_nt(p, vt), vt=vt)
            f = fc_ref[0, g, :, a:b] if kind == "fox" else None
            segs.append((a, b - a, f, functools.partial(lambda q, kt: _dot(q, kt), kt=kt), pv))
        kn = kn_ref[0, :, cols]
        vn = vn_ref[0, :, cols]
        f = fn_ref[0, g] if kind == "fox" else None
        segs.append((lc, ln, f, functools.partial(lambda q, kn: _dot_nt(q, kn), kn=kn),
                     functools.partial(lambda p, vn: _dot(p, vn), vn=vn)))

        s_cat = [seg[3](jnp.concatenate(qs, axis=0)) for seg in segs]
        ps, ls = [], []
        for i in range(2):
            ss = []
            for (start, length, f, _, _), sc in zip(segs, s_cat):
                kpos = start + lax.broadcasted_iota(jnp.int32, (1, length), 1)
                s = sc[i * tq:(i + 1) * tq]
                if kind == "diff":
                    s = s - sc_ref[g] * jnp.abs(qpos - kpos).astype(F32)
                    mask = ((kpos - N_META) >> CHUNK_SHIFT) <= ((qpos - N_META) >> CHUNK_SHIFT)
                else:
                    s = s - LOG2E * f[i:i + 1]
                    mask = kpos <= qpos
                ss.append(jnp.where(mask, s, NEG))
            m = functools.reduce(jnp.maximum, [jnp.max(s, axis=-1, keepdims=True) for s in ss])
            ps.append([jnp.exp2(s - m) for s in ss])
            ls.append(functools.reduce(jnp.add, [jnp.sum(p, axis=-1, keepdims=True) for p in ps[i]]))
        acc = functools.reduce(jnp.add, [seg[4](jnp.concatenate([pa, pb], axis=0).astype(BF16))
                                         for pa, pb, seg in zip(ps[0], ps[1], segs)])
        os_ = [acc[i * tq:(i + 1) * tq] / ls[i] for i in range(2)]
        if kind == "diff":
            o_ref[0, :, cols] = _diff_out(os_[0], os_[1], lam, lam_init, g_sub)
        else:
            lane = lax.broadcasted_iota(jnp.int32, os_[0].shape, 1)
            o_ref[0, :, cols] = jnp.where(lane < HEAD_DIM, os_[0], os_[1])


def _small_attention(kind, lam_init, layer, q, kc, vc, kn, vn, extra):
    nb, tq, _ = q.shape
    lc = 0 if kc is None else kc.shape[-1]
    ln = kn.shape[1]
    per_b = lambda rows: pl.BlockSpec((1, rows, HALF), lambda b: (b, 0, 0))
    cache = lambda a: pl.BlockSpec((1, 1) + a.shape[2:], lambda b: (layer, b, 0, 0))
    args, in_specs = [], []
    if kind == "diff":
        slopes, lam_vecs, g_sub = extra
        args += [slopes, lam_vecs]
        in_specs += [pl.BlockSpec(memory_space=pltpu.SMEM),
                     pl.BlockSpec((4, HEAD_DIM), lambda b: (0, 0))]
    args.append(q)
    in_specs.append(per_b(tq))
    if lc:
        args += [kc, vc]
        in_specs += [cache(kc), cache(vc)]
    args += [kn, vn]
    in_specs += [per_b(ln), per_b(ln)]
    if kind == "diff":
        args.append(g_sub)
        in_specs.append(pl.BlockSpec((1, GW), lambda b: (0, 0)))
    else:
        f_cache, f_new = extra
        if lc:
            args.append(f_cache)
            in_specs.append(pl.BlockSpec((1, GROUPS, 2, lc), lambda b: (b, 0, 0, 0)))
        args.append(f_new)
        in_specs.append(pl.BlockSpec((1, GROUPS, 2, ln), lambda b: (b, 0, 0, 0)))
    return pl.pallas_call(
        functools.partial(_small_attn_kernel, kind, tq, lc, ln, lam_init),
        out_shape=jax.ShapeDtypeStruct((nb, tq, HALF), F32),
        grid=(nb,),
        in_specs=in_specs,
        out_specs=per_b(tq),
        compiler_params=pltpu.CompilerParams(dimension_semantics=("arbitrary",),
                                             vmem_limit_bytes=VMEM_LIMIT),
        name="small_attn_" + kind,
    )(*args)


def _outproj_kernel(final, ad_ref, af_ref, gate_ref, x_ref, w_ref, gf_ref, y_ref):
    gt = gate_ref[...]
    sg = gt / (1.0 + jnp.exp(-gt))
    ud = (ad_ref[...] * sg[:, 0:HALF]).astype(BF16)
    uf = (af_ref[...] * sg[:, HALF:MIX]).astype(BF16)
    y = x_ref[...] + (_dot(ud, w_ref[0:HALF, :]) + _dot(uf, w_ref[HALF:MIX, :]))
    if final:
        y = _rms(y, gf_ref[...], NORM_EPS)
    y_ref[...] = y


def _out_project(final, layer, ad, af, gate, x, w_out, g_final):
    rows = x.shape[0]
    tm = math.gcd(rows, ROW_TILE)
    row = lambda i: (i, 0)
    fixed = lambda i: (0, 0)
    return pl.pallas_call(
        functools.partial(_outproj_kernel, final),
        out_shape=jax.ShapeDtypeStruct((rows, D_MODEL), F32),
        grid=(rows // tm,),
        in_specs=[pl.BlockSpec((tm, HALF), row), pl.BlockSpec((tm, HALF), row),
                  pl.BlockSpec((tm, MIX), row), pl.BlockSpec((tm, D_MODEL), row),
                  pl.BlockSpec((None, MIX, D_MODEL), lambda i: (layer, 0, 0)),
                  pl.BlockSpec((1, D_MODEL), fixed)],
        out_specs=pl.BlockSpec((tm, D_MODEL), row),
        compiler_params=pltpu.CompilerParams(dimension_semantics=("arbitrary",),
                                             vmem_limit_bytes=VMEM_LIMIT),
        name="outproj",
    )(ad, af, gate, x, w_out, g_final)


def kernel(x_prompt, x_sample, cache_diff_k, cache_diff_v, cache_fox_k, cache_fox_v, cache_fox_logf,
           meta_tokens, w_in, b_forget, norm_g, w_out, lambda_q1, lambda_k1, lambda_q2, lambda_k2,
           subln_g, final_norm_g):
    depth = w_in.shape[0]
    nb_p, seq, _ = x_prompt.shape
    nb_s, seq_s, _ = x_sample.shape
    lc = cache_diff_k.shape[2]

    c = [n * HALF for n in range(7)]
    fl0, g0 = c[6], c[6] + FOX_HEADS
    w_rows = jnp.concatenate([w_in[:, :, :fl0], w_in[:, :, g0:]], axis=-1).astype(BF16)
    w_f = jnp.pad(w_in[:, :, fl0:g0], ((0, 0), (0, 0), (0, LANES - FOX_HEADS))).astype(BF16)
    b_f = jnp.pad(b_forget.astype(F32), ((0, 0), (0, LANES - FOX_HEADS)))[:, None, :]
    w_cols = jnp.concatenate([w_in[:, :, c[0]:c[1]], w_in[:, :, c[2]:c[4]], w_in[:, :, c[5]:c[6]],
                              w_in[:, :, g0:]], axis=-1).astype(BF16)
    w_in_t = jnp.swapaxes(w_in, 1, 2)
    w_t = jnp.concatenate([w_in_t[:, c[1]:c[2]], w_in_t[:, c[4]:c[6]]], axis=1).astype(BF16)
    w_ft = jnp.pad(w_in_t[:, fl0:g0], ((0, 0), (0, FOX_HEADS), (0, 0))).astype(BF16)
    b_ft = b_forget.astype(F32)[:, :, None]
    w_o = w_out.astype(BF16)
    slopes = (2.0 ** (-8.0 * jnp.arange(1, DIFF_HEADS + 1, dtype=F32) / DIFF_HEADS)) * LOG2E
    g_final = final_norm_g.astype(F32).reshape(1, D_MODEL)
    g_layers = norm_g.astype(F32)[:, None, :]

    ckd_t = jnp.transpose(cache_diff_k, (0, 1, 3, 4, 5, 2)).reshape(depth, nb_s, HALF, lc)
    cvd = cache_diff_v.reshape(depth, nb_s, lc * DIFF_HEADS, GW)
    ckf_t = jnp.transpose(cache_fox_k, (0, 1, 3, 4, 2)).reshape(depth, nb_s, HALF, lc)
    cvf_t = jnp.transpose(cache_fox_v, (0, 1, 3, 4, 2)).reshape(depth, nb_s, HALF, lc)
    clf_t = jnp.swapaxes(cache_fox_logf, 2, 3).astype(F32)

    xf = x_prompt
    xm = jnp.broadcast_to(meta_tokens[None].astype(x_prompt.dtype),
                          (nb_p, N_META, D_MODEL)).reshape(nb_p * N_META, D_MODEL)
    xs = x_sample.reshape(nb_s * seq_s, D_MODEL)

    tp = N_META + seq
    stacks = (*(jnp.zeros((depth, nb_p, r, tp), F32) for r in (HALF, HALF, HALF, FOX_HEADS)),
              jnp.zeros((depth, nb_p, DIFF_HEADS * tp, GW), F32))
    rows_s = [[] for _ in range(5)]
    shape3 = lambda a, nb: a.reshape(nb, -1, a.shape[-1])
    for l in range(depth):
        lam_init = 0.8 - 0.6 * math.exp(-0.3 * l)
        final = l == depth - 1
        lam_vecs = jnp.stack([lambda_q1[l], lambda_k1[l], lambda_q2[l], lambda_k2[l]]).astype(F32)
        diff_extra = (slopes, lam_vecs, subln_g[l].astype(F32).reshape(1, GW))

        def out_project(ad, af, gate, x):
            return _out_project(final, l, ad.reshape(-1, HALF), af.reshape(-1, HALF), gate.reshape(-1, MIX),
                                x.reshape(-1, D_MODEL), w_o, g_final)

        (mqd, mkd32, mkd16, mvd32, mvd16, mqf, mkf32, mkf16, mvf32, mvf16, mlf, mgate) = [
            shape3(a, nb_p) for a in _project_rows(l, xm, g_layers, w_rows, w_f, b_f)]
        mlf_t = jnp.swapaxes(mlf[:, :, :FOX_HEADS], 1, 2)
        metas = (jnp.swapaxes(mkd32, 1, 2), jnp.swapaxes(mkf32, 1, 2), jnp.swapaxes(mvf32, 1, 2), mlf_t)
        (qd, kdt16, vd16, qf, kft16, vf16, lft, gate, *stacks) = _project_frames(
            l, xf, g_layers, w_cols, w_t, w_ft, b_ft, metas,
            mvd32.reshape(nb_p, DIFF_HEADS * N_META, GW), stacks)
        logf_t = jnp.concatenate([mlf_t, lft], axis=-1)
        f_all = _cum_logf(logf_t)
        f_meta = f_all[..., :N_META]
        f_frames = f_all[..., N_META:N_META + seq]

        ad = _frames_attention("diff", lam_init, qd, kdt16, vd16, mkd16, mvd16, diff_extra)
        af = _frames_attention("fox", lam_init, qf, kft16, vf16, mkf16, mvf16, (f_frames, f_meta))
        xf = out_project(ad, af, gate, xf).reshape(nb_p, seq, D_MODEL)
        if not final:
            mad = _small_attention("diff", lam_init, l, mqd, None, None, mkd16, mvd16, diff_extra)
            maf = _small_attention("fox", lam_init, l, mqf, None, None, mkf16, mvf16, (None, f_meta))
            xm = out_project(mad, maf, mgate, xm)

        (sqd, skd32, skd16, svd32, svd16, sqf, skf32, skf16, svf32, svf16, slf, sgate) = [
            shape3(a, nb_s) for a in _project_rows(l, xs, g_layers, w_rows, w_f, b_f)]
        logf_s = slf[:, :, :FOX_HEADS]
        f_s = _cum_logf(jnp.concatenate([clf_t[l], jnp.swapaxes(logf_s, 1, 2)], axis=-1))
        sad = _small_attention("diff", lam_init, l, sqd, ckd_t, cvd, skd16, svd16, diff_extra)
        saf = _small_attention("fox", lam_init, l, sqf, ckf_t, cvf_t, skf16, svf16,
                               (f_s[..., :lc], f_s[..., lc:lc + seq_s]))
        xs = out_project(sad, saf, sgate, xs)
        for dst, a in zip(rows_s, (skd32, svd32, skf32, svf32, logf_s)):
            dst.append(a)

    y_prompt = xf
    y_sample = xs.reshape(nb_s, seq_s, D_MODEL)
    skd, skf, svf, slf_stack, svd = stacks
    p_diff_k = jnp.transpose(skd.reshape(depth, nb_p, DIFF_HEADS, 2, HEAD_DIM, tp), (0, 1, 5, 2, 3, 4))
    p_diff_v = svd.reshape(depth, nb_p, tp, DIFF_HEADS, 2 * HEAD_DIM)
    p_fox_k = jnp.transpose(skf.reshape(depth, nb_p, FOX_HEADS, HEAD_DIM, tp), (0, 1, 4, 2, 3))
    p_fox_v = jnp.transpose(svf.reshape(depth, nb_p, FOX_HEADS, HEAD_DIM, tp), (0, 1, 4, 2, 3))
    p_fox_logf = jnp.swapaxes(slf_stack, 2, 3)
    s_diff_k = jnp.stack(rows_s[0]).reshape(depth, nb_s, seq_s, DIFF_HEADS, 2, HEAD_DIM)
    s_diff_v = jnp.stack(rows_s[1]).reshape(depth, nb_s, seq_s, DIFF_HEADS, 2 * HEAD_DIM)
    s_fox_k = jnp.stack(rows_s[2]).reshape(depth, nb_s, seq_s, FOX_HEADS, HEAD_DIM)
    s_fox_v = jnp.stack(rows_s[3]).reshape(depth, nb_s, seq_s, FOX_HEADS, HEAD_DIM)
    s_fox_logf = jnp.stack(rows_s[4])
    return (y_prompt, y_sample, p_diff_k, p_diff_v, p_fox_k, p_fox_v, p_fox_logf,
            s_diff_k, s_diff_v, s_fox_k, s_fox_v, s_fox_logf)
```

```python
import functools
import math

import jax
import jax.numpy as jnp
from jax import lax
from jax.experimental import pallas as pl
from jax.experimental.pallas import tpu as pltpu

D_MODEL = 1024
N_META = 16
CHUNK_SHIFT = 6
HEAD_DIM = 64
DIFF_HEADS = 4
FOX_HEADS = 8
GROUPS = 4
GW = 2 * HEAD_DIM
HALF = GROUPS * GW
MIX = 2 * HALF
NORM_EPS = 1e-6
SUBLN_EPS = 1e-5
NEG = -1e30
LOG2E = 1.4426950408889634
QSCALE = HEAD_DIM ** -0.5 * LOG2E
LANES = 128
CUM_CHUNK = 512
ATTN_BLOCK = 1024
ROW_TILE = 256
VMEM_LIMIT = 56 * 1024 * 1024

F32 = jnp.float32
BF16 = jnp.bfloat16


def _dot(a, b):
    return jnp.dot(a, b, preferred_element_type=F32)


def _dot_nt(a, b):
    return lax.dot_general(a, b, (((1,), (1,)), ((), ())), preferred_element_type=F32)


def _rms(x, g, eps):
    return (x * lax.rsqrt(jnp.mean(x * x, axis=-1, keepdims=True) + eps)) * g


def _log_sigmoid(z):
    return jnp.minimum(z, 0.0) - jnp.log(1.0 + jnp.exp(-jnp.abs(z)))


def _proj_rows_kernel(x_ref, g_ref, w_ref, wf_ref, bf_ref,
                      qd_ref, kd32_ref, kd16_ref, vd32_ref, vd16_ref,
                      qf_ref, kf32_ref, kf16_ref, vf32_ref, vf16_ref, lf_ref, gate_ref):
    hb = _rms(x_ref[...], g_ref[...], NORM_EPS).astype(BF16)

    def mm(c):
        return _dot(hb, w_ref[:, c * HALF:(c + 1) * HALF])

    qd_ref[...] = (mm(0) * QSCALE).astype(BF16)
    for c, r32, r16 in ((1, kd32_ref, kd16_ref), (2, vd32_ref, vd16_ref),
                        (4, kf32_ref, kf16_ref), (5, vf32_ref, vf16_ref)):
        z = mm(c)
        r32[...] = z
        r16[...] = z.astype(BF16)
    qf_ref[...] = (mm(3) * QSCALE).astype(BF16)
    gate_ref[:, 0:HALF] = mm(6).astype(BF16)
    gate_ref[:, HALF:MIX] = mm(7).astype(BF16)
    lf_ref[...] = _log_sigmoid(_dot(hb, wf_ref[...]) + bf_ref[...])


def _project_rows(layer, x, g, w_rows, w_f, b_f):
    rows = x.shape[0]
    tm = math.gcd(rows, ROW_TILE)
    row = lambda i: (i, 0)
    of_layer = lambda i: (layer, 0, 0)
    half16 = jax.ShapeDtypeStruct((rows, HALF), BF16)
    half32 = jax.ShapeDtypeStruct((rows, HALF), F32)
    out_shape = (half16, half32, half16, half32, half16,
                 half16, half32, half16, half32, half16,
                 jax.ShapeDtypeStruct((rows, LANES), F32),
                 jax.ShapeDtypeStruct((rows, MIX), BF16))
    return pl.pallas_call(
        _proj_rows_kernel,
        out_shape=out_shape,
        grid=(rows // tm,),
        in_specs=[pl.BlockSpec((tm, D_MODEL), row),
                  pl.BlockSpec((None, 1, D_MODEL), of_layer),
                  pl.BlockSpec((None, D_MODEL, 8 * HALF), of_layer),
                  pl.BlockSpec((None, D_MODEL, LANES), of_layer),
                  pl.BlockSpec((None, 1, LANES), of_layer)],
        out_specs=tuple(pl.BlockSpec((tm, s.shape[1]), row) for s in out_shape),
        compiler_params=pltpu.CompilerParams(dimension_semantics=("arbitrary",),
                                             vmem_limit_bytes=VMEM_LIMIT),
        name="proj_rows",
    )(x, g, w_rows, w_f, b_f)


def _proj_frames_kernel(nt, n_aliased, x_ref, g_ref, w_ref, wt_ref, wft_ref, bft_ref,
                        mkd_ref, mkf_ref, mvf_ref, mlf_ref, mvd_ref, *refs):
    (qd_ref, kdt16_ref, vd16_ref, qf_ref, kft16_ref, vf16_ref, lft_ref, gate_ref,
     skd_ref, skf_ref, svf_ref, slf_ref, svd_ref,
     ckd_ref, ckf_ref, cvf_ref, clf_ref, cvd_ref, stage_ref) = refs[n_aliased:]
    i = pl.program_id(1)
    tm = x_ref.shape[1]
    vrows = DIFF_HEADS * tm
    vtail = DIFF_HEADS * N_META
    shifted = ((ckd_ref, skd_ref, mkd_ref), (ckf_ref, skf_ref, mkf_ref), (cvf_ref, svf_ref, mvf_ref),
               (clf_ref, slf_ref, mlf_ref))

    @pl.when(i == 0)
    def _():
        for carry_ref, _, meta_ref in shifted:
            carry_ref[:, 0:N_META] = meta_ref[0]
        cvd_ref[...] = mvd_ref[0]

    def emit(z, carry_ref, out_ref):
        lane = lax.broadcasted_iota(jnp.int32, z.shape, 1)
        zr = pltpu.roll(z, N_META, 1)
        out_ref[0, 0] = jnp.where(lane < N_META, carry_ref[...], zr)
        carry_ref[...] = zr

    @pl.when(i < nt)
    def _():
        hb = _rms(x_ref[0], g_ref[...], NORM_EPS).astype(BF16)

        def mm(c):
            return _dot(hb, w_ref[:, c * HALF:(c + 1) * HALF])

        def mm_t(c):
            return _dot_nt(wt_ref[c * HALF:(c + 1) * HALF, :], hb)

        qd_ref[0] = (mm(0) * QSCALE).astype(BF16)
        z = mm(1)
        vd16_ref[0] = z.astype(BF16)
        for h in range(DIFF_HEADS):
            stage_ref[pl.ds(h, tm, stride=DIFF_HEADS), :] = z[:, h * GW:(h + 1) * GW]
        svd_ref[0, 0, 0:vtail, :] = cvd_ref[...]
        svd_ref[0, 0, vtail:vrows, :] = stage_ref[0:vrows - vtail, :]
        cvd_ref[...] = stage_ref[vrows - vtail:vrows, :]
        qf_ref[0] = (mm(2) * QSCALE).astype(BF16)
        vf16_ref[0] = mm(3).astype(BF16)
        gate_ref[0, :, 0:HALF] = mm(4).astype(BF16)
        gate_ref[0, :, HALF:MIX] = mm(5).astype(BF16)
        z = mm_t(0)
        kdt16_ref[0] = z.astype(BF16)
        emit(z, ckd_ref, skd_ref)
        z = mm_t(1)
        kft16_ref[0] = z.astype(BF16)
        emit(z, ckf_ref, skf_ref)
        emit(mm_t(2), cvf_ref, svf_ref)
        zf = _dot_nt(wft_ref[...], hb)
        lf = _log_sigmoid(zf[0:FOX_HEADS] + bft_ref[...])
        lft_ref[0] = lf
        emit(lf, clf_ref, slf_ref)

    @pl.when(i == nt)
    def _():
        for carry_ref, out_ref, _ in shifted:
            out_ref[0, 0] = carry_ref[...]
        svd_ref[0, 0, 0:vtail, :] = cvd_ref[...]


def _project_frames(layer, x, g, w_cols, w_t, w_ft, b_ft, metas, meta_vd, stacks):
    depth = g.shape[0]
    nb, t, _ = x.shape
    tm = ROW_TILE
    nt = t // tm
    last = nt - 1
    tp = N_META + t
    rows = lambda b, i: (b, jnp.minimum(i, last), 0)
    cols = lambda b, i: (b, 0, jnp.minimum(i, last))
    of_layer = lambda b, i: (layer, 0, 0)
    rm16 = (jax.ShapeDtypeStruct((nb, t, HALF), BF16), pl.BlockSpec((1, tm, HALF), rows))
    fm16 = (jax.ShapeDtypeStruct((nb, HALF, t), BF16), pl.BlockSpec((1, HALF, tm), cols))
    lft = (jax.ShapeDtypeStruct((nb, FOX_HEADS, t), F32), pl.BlockSpec((1, FOX_HEADS, tm), cols))
    gate = (jax.ShapeDtypeStruct((nb, t, MIX), BF16), pl.BlockSpec((1, tm, MIX), rows))
    stacked = [(jax.ShapeDtypeStruct((depth, nb, m.shape[1], tp), F32),
                pl.BlockSpec((1, 1, m.shape[1], tm), lambda b, i: (layer, b, 0, i))) for m in metas]
    stacked.append((jax.ShapeDtypeStruct((depth, nb, DIFF_HEADS * tp, GW), F32),
                    pl.BlockSpec((1, 1, DIFF_HEADS * tm, GW), lambda b, i: (layer, b, i, 0))))
    outs = (rm16, fm16, rm16, rm16, fm16, rm16, lft, gate, *stacked)
    aliased = () if stacks is None else tuple(stacks)
    first_stack_in = 7 + len(metas)
    first_stack_out = len(outs) - len(stacked)
    return pl.pallas_call(
        functools.partial(_proj_frames_kernel, nt, len(aliased)),
        out_shape=tuple(o[0] for o in outs),
        grid=(nb, nt + 1),
        in_specs=[pl.BlockSpec((1, tm, D_MODEL), rows),
                  pl.BlockSpec((None, 1, D_MODEL), of_layer),
                  pl.BlockSpec((None, D_MODEL, 6 * HALF), of_layer),
                  pl.BlockSpec((None, 3 * HALF, D_MODEL), of_layer),
                  pl.BlockSpec((None, 2 * FOX_HEADS, D_MODEL), of_layer),
                  pl.BlockSpec((None, FOX_HEADS, 1), of_layer),
                  *[pl.BlockSpec((1, m.shape[1], N_META), lambda b, i: (b, 0, 0)) for m in metas],
                  pl.BlockSpec((1, DIFF_HEADS * N_META, GW), lambda b, i: (b, 0, 0)),
                  *[pl.BlockSpec(memory_space=pl.ANY) for _ in aliased]],
        out_specs=tuple(o[1] for o in outs),
        scratch_shapes=[*[pltpu.VMEM((m.shape[1], tm), F32) for m in metas],
                        pltpu.VMEM((DIFF_HEADS * N_META, GW), F32),
                        pltpu.VMEM((DIFF_HEADS * tm, GW), F32)],
        input_output_aliases={first_stack_in + k: first_stack_out + k for k in range(len(aliased))},
        compiler_params=pltpu.CompilerParams(dimension_semantics=("arbitrary", "arbitrary"),
                                             vmem_limit_bytes=VMEM_LIMIT),
        name="proj_frames",
    )(x, g, w_cols, w_t, w_ft, b_ft, *metas, meta_vd, *aliased)


def _cumsum_kernel(xa_ref, xb_ref, oa_ref, ob_ref):
    rows = xa_ref.shape[0]
    tris = {}

    def tri(n):
        if n not in tris:
            r = lax.broadcasted_iota(jnp.int32, (n, n), 0)
            c = lax.broadcasted_iota(jnp.int32, (n, n), 1)
            tris[n] = (r <= c).astype(BF16)
        return tris[n]

    carry = jnp.zeros((rows, 1), F32)
    for x_ref, o_ref in ((xa_ref, oa_ref), (xb_ref, ob_ref)):
        cols = x_ref.shape[1]
        for start in range(0, cols, CUM_CHUNK):
            w = min(CUM_CHUNK, cols - start)
            x = x_ref[:, start:start + w]
            hi = x.astype(BF16)
            r1 = x - hi.astype(F32)
            mid = r1.astype(BF16)
            lo = (r1 - mid.astype(F32)).astype(BF16)
            y = (_dot(hi, tri(w)) + _dot(mid, tri(w))) + _dot(lo, tri(w)) + carry
            o_ref[:, start:start + w] = y
            carry = y[:, w - 1:w]


def _cum_logf(layer, lead, new):
    nb, h, lb = new.shape
    la = lead.shape[-1]
    rows = nb * h
    if lead.ndim == 4:
        lead = lead.reshape(lead.shape[0], rows, la)
        lead_spec = pl.BlockSpec((None, rows, la), lambda i: (layer, 0, 0))
    else:
        lead = lead.reshape(rows, la)
        lead_spec = pl.BlockSpec((rows, la), lambda i: (0, 0))
    fa, fb = pl.pallas_call(
        _cumsum_kernel,
        out_shape=(jax.ShapeDtypeStruct((rows, la), F32), jax.ShapeDtypeStruct((rows, lb), F32)),
        grid=(1,),
        in_specs=[lead_spec, pl.BlockSpec((rows, lb), lambda i: (0, 0))],
        out_specs=(pl.BlockSpec((rows, la), lambda i: (0, 0)), pl.BlockSpec((rows, lb), lambda i: (0, 0))),
        compiler_params=pltpu.CompilerParams(dimension_semantics=("arbitrary",),
                                             vmem_limit_bytes=VMEM_LIMIT),
        name="cumsum",
    )(lead, new.reshape(rows, lb))
    return fa.reshape(nb, h, la), fb.reshape(nb, h, lb)


def _lambda(lam_ref, lam_init):
    lv = lam_ref[...]
    a = jnp.sum(lv[0:1] * lv[1:2], axis=-1, keepdims=True)
    b = jnp.sum(lv[2:3] * lv[3:4], axis=-1, keepdims=True)
    return jnp.exp(a) - jnp.exp(b) + lam_init


def _pick_row(x, r):
    row = lax.broadcasted_iota(jnp.int32, x.shape, 0)
    return jnp.sum(jnp.where(row == r, x, 0.0), axis=0, keepdims=True)


def _split_lanes(x, fill):
    lane = lax.broadcasted_iota(jnp.int32, x.shape, 1)
    f = jnp.full_like(x, fill)
    return jnp.where(lane < HEAD_DIM, x, f), jnp.where(lane >= HEAD_DIM, x, f)


def _diff_out(o1, o2, lam, lam_init, g_sub):
    return _rms(o1 - lam * o2, g_sub, SUBLN_EPS) * (1.0 - lam_init)


def _frames_attn_kernel(kind, tq, lam_init, *refs):
    if kind == "diff":
        (sc_ref, lam_ref, q_ref, kt_ref, v_ref, mk_ref, mv_ref, g_ref, o_ref,
         sa_ref, sb_ref, acc_ref, diag_ref) = refs
    else:
        q_ref, kt_ref, v_ref, mk_ref, mv_ref, fk_ref, fm_ref, o_ref, sa_ref, sb_ref, acc_ref, diag_ref = refs
    gidx = pl.program_id(1)
    qi = pl.program_id(2)
    qs = _split_lanes(q_ref[0], 0)
    s_refs = (sa_ref, sb_ref)
    if kind == "diff":
        slope = sc_ref[gidx]

    @pl.when(qi == 0)
    def _():
        qr = lax.broadcasted_iota(jnp.int32, (tq, tq), 0)
        kr = lax.broadcasted_iota(jnp.int32, (tq, tq), 1)
        if kind == "diff":
            tile = slope * (2 * jnp.minimum(qr, kr) - kr).astype(F32)
            mask = (kr >> CHUNK_SHIFT) <= (qr >> CHUNK_SHIFT)
        else:
            tile = jnp.zeros((tq, tq), F32)
            mask = kr <= qr
        diag_ref[...] = jnp.where(mask, tile, NEG)

    def values(v):
        if kind == "diff":
            ve = jnp.concatenate([v, jnp.ones_like(v)], axis=1)
            return ve, ve
        return _split_lanes(v, 1)

    def produce(n, slot, diag):
        ks = pl.multiple_of(n * tq, tq)
        ktb = kt_ref[0, :, pl.ds(ks, tq)]
        if kind == "diff":
            if diag:
                bias = [diag_ref[...]] * 2
            else:
                kpos = lax.broadcasted_iota(jnp.int32, (1, tq), 1) + (n - qi) * tq
                bias = [slope * kpos.astype(F32)] * 2
        else:
            fk = fk_ref[0, :, pl.ds(ks, tq)]
            bias = [-LOG2E * _pick_row(fk, 2 * gidx), -LOG2E * _pick_row(fk, 2 * gidx + 1)]
        bms = []
        for i in range(2):
            s = _dot(qs[i], ktb) + bias[i]
            if diag and kind == "fox":
                s = s + diag_ref[...]
            s_refs[slot][i] = s
            bms.append(jnp.max(s, axis=-1, keepdims=True))
        return tuple(bms)

    def consume(n, slot, ms, bms):
        ks = pl.multiple_of(n * tq, tq)
        vs = values(v_ref[0, pl.ds(ks, tq), :])
        out = []
        for i in range(2):
            m_new = jnp.maximum(ms[i], bms[i])
            alpha = jnp.exp2(ms[i] - m_new)
            p = jnp.exp2(s_refs[slot][i] - m_new)
            acc_ref[i] = alpha * acc_ref[i] + _dot(p.astype(BF16), vs[i])
            out.append(m_new)
        return tuple(out)

    mv = mv_ref[0]
    if kind == "diff":
        kpos = lax.broadcasted_iota(jnp.int32, (1, N_META), 1) - (N_META + qi * tq)
        bias_m = [slope * kpos.astype(F32)] * 2
    else:
        fm = fm_ref[0]
        bias_m = [-LOG2E * _pick_row(fm, 2 * gidx), -LOG2E * _pick_row(fm, 2 * gidx + 1)]
    s = _dot_nt(jnp.concatenate(qs, axis=0), mk_ref[0])
    ms, ps = [], []
    for i in range(2):
        si = s[i * tq:(i + 1) * tq] + bias_m[i]
        m = jnp.max(si, axis=-1, keepdims=True)
        ps.append(jnp.exp2(si - m).astype(BF16))
        ms.append(m)
    r = _dot(jnp.concatenate(ps, axis=0), jnp.concatenate([mv, jnp.ones_like(mv)], axis=1))
    if kind == "diff":
        acc_ref[0] = r[:tq]
        acc_ref[1] = r[tq:]
    else:
        lane = lax.broadcasted_iota(jnp.int32, (tq, GW), 1)
        acc_ref[0] = jnp.where(lane < HEAD_DIM, r[:tq, :GW], r[:tq, GW:])
        acc_ref[1] = jnp.where(lane >= HEAD_DIM, r[tq:, :GW], r[tq:, GW:])
    ms = tuple(ms)

    bm0 = produce(qi, 0, True)

    def pair(t, c):
        ms, bm0 = c[:2], c[2:]
        bm1 = produce(2 * t, 1, False)
        ms = consume(jnp.where(t == 0, qi, 2 * t - 1), 0, ms, bm0)
        bm0 = produce(2 * t + 1, 0, False)
        ms = consume(2 * t, 1, ms, bm1)
        return ms + bm0

    c = lax.fori_loop(0, qi // 2, pair, ms + bm0)
    ms, bm0 = c[:2], c[2:]
    pending = jnp.where(qi < 2, qi, 2 * (qi // 2) - 1)

    def odd(ms, bm0):
        bm1 = produce(qi - 1, 1, False)
        ms = consume(pending, 0, ms, bm0)
        return consume(qi - 1, 1, ms, bm1)

    def even(ms, bm0):
        return consume(pending, 0, ms, bm0)

    lax.cond(qi % 2 == 1, odd, even, ms, bm0)

    a = acc_ref[0]
    b = acc_ref[1]
    if kind == "diff":
        out = _diff_out(a[:, :GW] / a[:, GW:], b[:, :GW] / b[:, GW:],
                        _lambda(lam_ref, lam_init), lam_init, g_ref[...])
    else:
        lane = lax.broadcasted_iota(jnp.int32, a.shape, 1)
        out = jnp.where(lane < HEAD_DIM, a / pltpu.roll(a, HEAD_DIM, 1), b / pltpu.roll(b, HEAD_DIM, 1))
    o_ref[0] = out.astype(o_ref.dtype)


def _frames_attention(kind, lam_init, q, kt, v, mk, mv, extra):
    nb, t, _ = q.shape
    tq = ATTN_BLOCK
    idx = lambda b, g, i: (b, i, g)
    rows_g = lambda b, g, i: (b, 0, g)
    feat_g = lambda b, g, i: (b, g, 0)
    fixed = lambda b, g, i: (0, 0)
    qspec = pl.BlockSpec((1, tq, GW), idx)
    common = [qspec, pl.BlockSpec((1, GW, t), feat_g), pl.BlockSpec((1, t, GW), rows_g),
              pl.BlockSpec((1, N_META, GW), rows_g), pl.BlockSpec((1, N_META, GW), rows_g)]
    if kind == "diff":
        slopes, lam_vecs, g_sub = extra
        args = (slopes, lam_vecs, q, kt, v, mk, mv, g_sub)
        in_specs = [pl.BlockSpec(memory_space=pltpu.SMEM), pl.BlockSpec((4, HEAD_DIM), fixed),
                    *common, pl.BlockSpec((1, GW), fixed)]
        acc_w = 2 * GW
    else:
        f_frames, f_meta = extra
        args = (q, kt, v, mk, mv, f_frames, f_meta)
        in_specs = [*common,
                    pl.BlockSpec((1, FOX_HEADS, t), lambda b, g, i: (b, 0, 0)),
                    pl.BlockSpec((1, FOX_HEADS, N_META), lambda b, g, i: (b, 0, 0))]
        acc_w = GW
    return pl.pallas_call(
        functools.partial(_frames_attn_kernel, kind, tq, lam_init),
        out_shape=jax.ShapeDtypeStruct((nb, t, HALF), BF16),
        grid=(nb, GROUPS, t // tq),
        in_specs=in_specs,
        out_specs=qspec,
        scratch_shapes=[pltpu.VMEM((2, tq, tq), F32), pltpu.VMEM((2, tq, tq), F32),
                        pltpu.VMEM((2, tq, acc_w), F32), pltpu.VMEM((tq, tq), F32)],
        compiler_params=pltpu.CompilerParams(
            dimension_semantics=("arbitrary", "arbitrary", "arbitrary"),
            vmem_limit_bytes=VMEM_LIMIT),
        name="frames_attn_" + kind,
    )(*args)


def _small_attn_kernel(kind, tq, lc, ln, lam_init, *refs):
    refs = list(refs)
    o_ref = refs.pop()
    if kind == "diff":
        sc_ref, lam_ref = refs[0:2]
        refs = refs[2:]
    q_ref = refs.pop(0)
    if lc:
        kc_ref, vc_ref = refs[0:2]
        refs = refs[2:]
    kn_ref, vn_ref = refs[0:2]
    refs = refs[2:]
    if kind == "diff":
        g_ref, = refs
        lam = _lambda(lam_ref, lam_init)
        g_sub = g_ref[...]
    else:
        if lc:
            fc_ref = refs.pop(0)
        fn_ref, = refs

    lc_main = (lc // LANES) * LANES
    spans = [sp for sp in ((0, lc_main), (lc_main, lc)) if sp[1] > sp[0]]
    qpos = lc + lax.broadcasted_iota(jnp.int32, (tq, 1), 0)

    for g in range(GROUPS):
        cols = slice(g * GW, (g + 1) * GW)
        qs = _split_lanes(q_ref[0, :, cols], 0)
        segs = []
        for a, b in spans:
            kt = kc_ref[0, 0, cols, a:b].astype(BF16)
            if kind == "diff":
                vb = vc_ref[0, 0, pl.ds(a * DIFF_HEADS + g, b - a, stride=DIFF_HEADS), :].astype(BF16)
                pv = functools.partial(lambda p, vb: _dot(p, vb), vb=vb)
            else:
                vt = vc_ref[0, 0, cols, a:b].astype(BF16)
                pv = functools.partial(lambda p, vt: _dot_nt(p, vt), vt=vt)
            f = fc_ref[0, 2 * g:2 * g + 2, a:b] if kind == "fox" else None
            segs.append((a, b - a, f, functools.partial(lambda q, kt: _dot(q, kt), kt=kt), pv))
        kn = kn_ref[0, :, cols]
        vn = vn_ref[0, :, cols]
        f = fn_ref[0, 2 * g:2 * g + 2, :] if kind == "fox" else None
        segs.append((lc, ln, f, functools.partial(lambda q, kn: _dot_nt(q, kn), kn=kn),
                     functools.partial(lambda p, vn: _dot(p, vn), vn=vn)))

        s_cat = [seg[3](jnp.concatenate(qs, axis=0)) for seg in segs]
        ps, ls = [], []
        for i in range(2):
            ss = []
            for (start, length, f, _, _), sc in zip(segs, s_cat):
                kpos = start + lax.broadcasted_iota(jnp.int32, (1, length), 1)
                s = sc[i * tq:(i + 1) * tq]
                if kind == "diff":
                    s = s - sc_ref[g] * jnp.abs(qpos - kpos).astype(F32)
                    mask = ((kpos - N_META) >> CHUNK_SHIFT) <= ((qpos - N_META) >> CHUNK_SHIFT)
                else:
                    s = s - LOG2E * f[i:i + 1]
                    mask = kpos <= qpos
                ss.append(jnp.where(mask, s, NEG))
            m = functools.reduce(jnp.maximum, [jnp.max(s, axis=-1, keepdims=True) for s in ss])
            ps.append([jnp.exp2(s - m) for s in ss])
            ls.append(functools.reduce(jnp.add, [jnp.sum(p, axis=-1, keepdims=True) for p in ps[i]]))
        acc = functools.reduce(jnp.add, [seg[4](jnp.concatenate([pa, pb], axis=0).astype(BF16))
                                         for pa, pb, seg in zip(ps[0], ps[1], segs)])
        os_ = [acc[i * tq:(i + 1) * tq] / ls[i] for i in range(2)]
        if kind == "diff":
            out = _diff_out(os_[0], os_[1], lam, lam_init, g_sub)
        else:
            lane = lax.broadcasted_iota(jnp.int32, os_[0].shape, 1)
            out = jnp.where(lane < HEAD_DIM, os_[0], os_[1])
        o_ref[0, :, cols] = out.astype(o_ref.dtype)


def _small_attention(kind, lam_init, layer, q, kc, vc, kn, vn, extra):
    nb, tq, _ = q.shape
    lc = 0 if kc is None else kc.shape[-1]
    ln = kn.shape[1]
    per_b = lambda rows: pl.BlockSpec((1, rows, HALF), lambda b: (b, 0, 0))
    cache = lambda a: pl.BlockSpec((1, 1) + a.shape[2:], lambda b: (layer, b, 0, 0))
    args, in_specs = [], []
    if kind == "diff":
        slopes, lam_vecs, g_sub = extra
        args += [slopes, lam_vecs]
        in_specs += [pl.BlockSpec(memory_space=pltpu.SMEM),
                     pl.BlockSpec((4, HEAD_DIM), lambda b: (0, 0))]
    args.append(q)
    in_specs.append(per_b(tq))
    if lc:
        args += [kc, vc]
        in_specs += [cache(kc), cache(vc)]
    args += [kn, vn]
    in_specs += [per_b(ln), per_b(ln)]
    if kind == "diff":
        args.append(g_sub)
        in_specs.append(pl.BlockSpec((1, GW), lambda b: (0, 0)))
    else:
        f_cache, f_new = extra
        if lc:
            args.append(f_cache)
            in_specs.append(pl.BlockSpec((1, FOX_HEADS, lc), lambda b: (b, 0, 0)))
        args.append(f_new)
        in_specs.append(pl.BlockSpec((1, FOX_HEADS, ln), lambda b: (b, 0, 0)))
    return pl.pallas_call(
        functools.partial(_small_attn_kernel, kind, tq, lc, ln, lam_init),
        out_shape=jax.ShapeDtypeStruct((nb, tq, HALF), BF16),
        grid=(nb,),
        in_specs=in_specs,
        out_specs=per_b(tq),
        compiler_params=pltpu.CompilerParams(dimension_semantics=("arbitrary",),
                                             vmem_limit_bytes=VMEM_LIMIT),
        name="small_attn_" + kind,
    )(*args)


def _outproj_kernel(final, ad_ref, af_ref, gate_ref, x_ref, w_ref, gf_ref, y_ref):
    gt = gate_ref[...].astype(F32)
    sg = gt / (1.0 + jnp.exp(-gt))
    ud = (ad_ref[...].astype(F32) * sg[:, 0:HALF]).astype(BF16)
    uf = (af_ref[...].astype(F32) * sg[:, HALF:MIX]).astype(BF16)
    y = x_ref[...] + (_dot(ud, w_ref[0:HALF, :]) + _dot(uf, w_ref[HALF:MIX, :]))
    if final:
        y = _rms(y, gf_ref[...], NORM_EPS)
    y_ref[...] = y


def _out_project(final, layer, ad, af, gate, x, w_out, g_final):
    rows = x.shape[0]
    tm = math.gcd(rows, ROW_TILE)
    row = lambda i: (i, 0)
    fixed = lambda i: (0, 0)
    return pl.pallas_call(
        functools.partial(_outproj_kernel, final),
        out_shape=jax.ShapeDtypeStruct((rows, D_MODEL), F32),
        grid=(rows // tm,),
        in_specs=[pl.BlockSpec((tm, HALF), row), pl.BlockSpec((tm, HALF), row),
                  pl.BlockSpec((tm, MIX), row), pl.BlockSpec((tm, D_MODEL), row),
                  pl.BlockSpec((None, MIX, D_MODEL), lambda i: (layer, 0, 0)),
                  pl.BlockSpec((1, D_MODEL), fixed)],
        out_specs=pl.BlockSpec((tm, D_MODEL), row),
        compiler_params=pltpu.CompilerParams(dimension_semantics=("arbitrary",),
                                             vmem_limit_bytes=VMEM_LIMIT),
        name="outproj",
    )(ad, af, gate, x, w_out, g_final)


def kernel(x_prompt, x_sample, cache_diff_k, cache_diff_v, cache_fox_k, cache_fox_v, cache_fox_logf,
           meta_tokens, w_in, b_forget, norm_g, w_out, lambda_q1, lambda_k1, lambda_q2, lambda_k2,
           subln_g, final_norm_g):
    depth = w_in.shape[0]
    nb_p, seq, _ = x_prompt.shape
    nb_s, seq_s, _ = x_sample.shape
    lc = cache_diff_k.shape[2]

    c = [n * HALF for n in range(7)]
    fl0, g0 = c[6], c[6] + FOX_HEADS
    w_rows = jnp.concatenate([w_in[:, :, :fl0], w_in[:, :, g0:]], axis=-1).astype(BF16)
    w_f = jnp.pad(w_in[:, :, fl0:g0], ((0, 0), (0, 0), (0, LANES - FOX_HEADS))).astype(BF16)
    b_f = jnp.pad(b_forget.astype(F32), ((0, 0), (0, LANES - FOX_HEADS)))[:, None, :]
    w_cols = jnp.concatenate([w_in[:, :, c[0]:c[1]], w_in[:, :, c[2]:c[4]], w_in[:, :, c[5]:c[6]],
                              w_in[:, :, g0:]], axis=-1).astype(BF16)
    w_in_t = jnp.swapaxes(w_in, 1, 2)
    w_t = jnp.concatenate([w_in_t[:, c[1]:c[2]], w_in_t[:, c[4]:c[6]]], axis=1).astype(BF16)
    w_ft = jnp.pad(w_in_t[:, fl0:g0], ((0, 0), (0, FOX_HEADS), (0, 0))).astype(BF16)
    b_ft = b_forget.astype(F32)[:, :, None]
    w_o = w_out.astype(BF16)
    slopes = (2.0 ** (-8.0 * jnp.arange(1, DIFF_HEADS + 1, dtype=F32) / DIFF_HEADS)) * LOG2E
    g_final = final_norm_g.astype(F32).reshape(1, D_MODEL)
    g_layers = norm_g.astype(F32)[:, None, :]

    ckd_t = jnp.transpose(cache_diff_k, (0, 1, 3, 4, 5, 2)).reshape(depth, nb_s, HALF, lc)
    cvd = cache_diff_v.reshape(depth, nb_s, lc * DIFF_HEADS, GW)
    ckf_t = jnp.transpose(cache_fox_k, (0, 1, 3, 4, 2)).reshape(depth, nb_s, HALF, lc)
    cvf_t = jnp.transpose(cache_fox_v, (0, 1, 3, 4, 2)).reshape(depth, nb_s, HALF, lc)
    clf_t = jnp.swapaxes(cache_fox_logf, 2, 3).astype(F32)

    xf = x_prompt
    xm = jnp.broadcast_to(meta_tokens[None].astype(x_prompt.dtype),
                          (nb_p, N_META, D_MODEL)).reshape(nb_p * N_META, D_MODEL)
    xs = x_sample.reshape(nb_s * seq_s, D_MODEL)

    tp = N_META + seq
    stacks = (*(jnp.zeros((depth, nb_p, r, tp), F32) for r in (HALF, HALF, HALF, FOX_HEADS)),
              jnp.zeros((depth, nb_p, DIFF_HEADS * tp, GW), F32))
    rows_s = [[] for _ in range(5)]
    shape3 = lambda a, nb: a.reshape(nb, -1, a.shape[-1])
    for l in range(depth):
        lam_init = 0.8 - 0.6 * math.exp(-0.3 * l)
        final = l == depth - 1
        lam_vecs = jnp.stack([lambda_q1[l], lambda_k1[l], lambda_q2[l], lambda_k2[l]]).astype(F32)
        diff_extra = (slopes, lam_vecs, subln_g[l].astype(F32).reshape(1, GW))

        def out_project(ad, af, gate, x):
            return _out_project(final, l, ad.reshape(-1, HALF), af.reshape(-1, HALF), gate.reshape(-1, MIX),
                                x.reshape(-1, D_MODEL), w_o, g_final)

        (mqd, mkd32, mkd16, mvd32, mvd16, mqf, mkf32, mkf16, mvf32, mvf16, mlf, mgate) = [
            shape3(a, nb_p) for a in _project_rows(l, xm, g_layers, w_rows, w_f, b_f)]
        mlf_t = jnp.swapaxes(mlf[:, :, :FOX_HEADS], 1, 2)
        metas = (jnp.swapaxes(mkd32, 1, 2), jnp.swapaxes(mkf32, 1, 2), jnp.swapaxes(mvf32, 1, 2), mlf_t)
        (qd, kdt16, vd16, qf, kft16, vf16, lft, gate, *stacks) = _project_frames(
            l, xf, g_layers, w_cols, w_t, w_ft, b_ft, metas,
            mvd32.reshape(nb_p, DIFF_HEADS * N_META, GW), stacks)
        f_meta, f_frames = _cum_logf(l, mlf_t, lft)

        ad = _frames_attention("diff", lam_init, qd, kdt16, vd16, mkd16, mvd16, diff_extra)
        af = _frames_attention("fox", lam_init, qf, kft16, vf16, mkf16, mvf16, (f_frames, f_meta))
        xf = out_project(ad, af, gate, xf).reshape(nb_p, seq, D_MODEL)
        if not final:
            mad = _small_attention("diff", lam_init, l, mqd, None, None, mkd16, mvd16, diff_extra)
            maf = _small_attention("fox", lam_init, l, mqf, None, None, mkf16, mvf16, (None, f_meta))
            xm = out_project(mad, maf, mgate, xm)

        (sqd, skd32, skd16, svd32, svd16, sqf, skf32, skf16, svf32, svf16, slf, sgate) = [
            shape3(a, nb_s) for a in _project_rows(l, xs, g_layers, w_rows, w_f, b_f)]
        logf_s = slf[:, :, :FOX_HEADS]
        f_s = _cum_logf(l, clf_t, jnp.swapaxes(logf_s, 1, 2))
        sad = _small_attention("diff", lam_init, l, sqd, ckd_t, cvd, skd16, svd16, diff_extra)
        saf = _small_attention("fox", lam_init, l, sqf, ckf_t, cvf_t, skf16, svf16, f_s)
        xs = out_project(sad, saf, sgate, xs)
        for dst, a in zip(rows_s, (skd32, svd32, skf32, svf32, logf_s)):
            dst.append(a)

    y_prompt = xf
    y_sample = xs.reshape(nb_s, seq_s, D_MODEL)
    skd, skf, svf, slf_stack, svd = stacks
    p_diff_k = jnp.transpose(skd.reshape(depth, nb_p, DIFF_HEADS, 2, HEAD_DIM, tp), (0, 1, 5, 2, 3, 4))
    p_diff_v = svd.reshape(depth, nb_p, tp, DIFF_HEADS, 2 * HEAD_DIM)
    p_fox_k = jnp.transpose(skf.reshape(depth, nb_p, FOX_HEADS, HEAD_DIM, tp), (0, 1, 4, 2, 3))
    p_fox_v = jnp.transpose(svf.reshape(depth, nb_p, FOX_HEADS, HEAD_DIM, tp), (0, 1, 4, 2, 3))
    p_fox_logf = jnp.swapaxes(slf_stack, 2, 3)
    s_diff_k = jnp.stack(rows_s[0]).reshape(depth, nb_s, seq_s, DIFF_HEADS, 2, HEAD_DIM)
    s_diff_v = jnp.stack(rows_s[1]).reshape(depth, nb_s, seq_s, DIFF_HEADS, 2 * HEAD_DIM)
    s_fox_k = jnp.stack(rows_s[2]).reshape(depth, nb_s, seq_s, FOX_HEADS, HEAD_DIM)
    s_fox_v = jnp.stack(rows_s[3]).reshape(depth, nb_s, seq_s, FOX_HEADS, HEAD_DIM)
    s_fox_logf = jnp.stack(rows_s[4])
    return (y_prompt, y_sample, p_diff_k, p_diff_v, p_fox_k, p_fox_v, p_fox_logf,
            s_diff_k, s_diff_v, s_fox_k, s_fox_v, s_fox_logf)
```

```python
import functools
import math

import jax
import jax.numpy as jnp
from jax import lax
from jax.experimental import pallas as pl
from jax.experimental.pallas import tpu as pltpu

D_MODEL = 1024
N_META = 16
CHUNK_SHIFT = 6
HEAD_DIM = 64
DIFF_HEADS = 4
FOX_HEADS = 8
GROUPS = 4
GW = 2 * HEAD_DIM
HALF = GROUPS * GW
MIX = 2 * HALF
NORM_EPS = 1e-6
SUBLN_EPS = 1e-5
NEG = -1e30
LOG2E = 1.4426950408889634
QSCALE = HEAD_DIM ** -0.5 * LOG2E
LANES = 128
CUM_CHUNK = 512
ATTN_BLOCK = 1024
ROW_TILE = 512
VMEM_LIMIT = 56 * 1024 * 1024

F32 = jnp.float32
BF16 = jnp.bfloat16


def _dot(a, b):
    return jnp.dot(a, b, preferred_element_type=F32)


def _dot_nt(a, b):
    return lax.dot_general(a, b, (((1,), (1,)), ((), ())), preferred_element_type=F32)


def _rms(x, g, eps):
    return (x * lax.rsqrt(jnp.mean(x * x, axis=-1, keepdims=True) + eps)) * g


def _log_sigmoid(z):
    return jnp.minimum(z, 0.0) - jnp.log(1.0 + jnp.exp(-jnp.abs(z)))


def _proj_rows_kernel(x_ref, g_ref, w_ref, wf_ref, bf_ref,
                      qd_ref, kd32_ref, kd16_ref, vd32_ref, vd16_ref,
                      qf_ref, kf32_ref, kf16_ref, vf32_ref, vf16_ref, lf_ref, gate_ref):
    hb = _rms(x_ref[...], g_ref[...], NORM_EPS).astype(BF16)

    def mm(c):
        return _dot(hb, w_ref[:, c * HALF:(c + 1) * HALF])

    qd_ref[...] = (mm(0) * QSCALE).astype(BF16)
    for c, r32, r16 in ((1, kd32_ref, kd16_ref), (2, vd32_ref, vd16_ref),
                        (4, kf32_ref, kf16_ref), (5, vf32_ref, vf16_ref)):
        z = mm(c)
        r32[...] = z
        r16[...] = z.astype(BF16)
    qf_ref[...] = (mm(3) * QSCALE).astype(BF16)
    gate_ref[:, 0:HALF] = mm(6).astype(BF16)
    gate_ref[:, HALF:MIX] = mm(7).astype(BF16)
    lf_ref[...] = _log_sigmoid(_dot(hb, wf_ref[...]) + bf_ref[...])


def _project_rows(layer, x, g, w_rows, w_f, b_f):
    rows = x.shape[0]
    tm = math.gcd(rows, ROW_TILE)
    row = lambda i: (i, 0)
    of_layer = lambda i: (layer, 0, 0)
    half16 = jax.ShapeDtypeStruct((rows, HALF), BF16)
    half32 = jax.ShapeDtypeStruct((rows, HALF), F32)
    out_shape = (half16, half32, half16, half32, half16,
                 half16, half32, half16, half32, half16,
                 jax.ShapeDtypeStruct((rows, LANES), F32),
                 jax.ShapeDtypeStruct((rows, MIX), BF16))
    return pl.pallas_call(
        _proj_rows_kernel,
        out_shape=out_shape,
        grid=(rows // tm,),
        in_specs=[pl.BlockSpec((tm, D_MODEL), row),
                  pl.BlockSpec((None, 1, D_MODEL), of_layer),
                  pl.BlockSpec((None, D_MODEL, 8 * HALF), of_layer),
                  pl.BlockSpec((None, D_MODEL, LANES), of_layer),
                  pl.BlockSpec((None, 1, LANES), of_layer)],
        out_specs=tuple(pl.BlockSpec((tm, s.shape[1]), row) for s in out_shape),
        compiler_params=pltpu.CompilerParams(dimension_semantics=("arbitrary",),
                                             vmem_limit_bytes=VMEM_LIMIT),
        name="proj_rows",
    )(x, g, w_rows, w_f, b_f)


def _proj_frames_kernel(nt, n_aliased, x_ref, g_ref, w_ref, wt_ref, wft_ref, bft_ref,
                        mkd_ref, mkf_ref, mvf_ref, mlf_ref, mvd_ref, *refs):
    (qd_ref, kdt16_ref, vd16_ref, qf_ref, kft16_ref, vf16_ref, lft_ref, gate_ref,
     skd_ref, skf_ref, svf_ref, slf_ref, svd_ref,
     ckd_ref, ckf_ref, cvf_ref, clf_ref, cvd_ref, stage_ref) = refs[n_aliased:]
    i = pl.program_id(1)
    tm = x_ref.shape[1]
    vrows = DIFF_HEADS * tm
    vtail = DIFF_HEADS * N_META
    shifted = ((ckd_ref, skd_ref, mkd_ref), (ckf_ref, skf_ref, mkf_ref), (cvf_ref, svf_ref, mvf_ref),
               (clf_ref, slf_ref, mlf_ref))

    @pl.when(i == 0)
    def _():
        for carry_ref, _, meta_ref in shifted:
            carry_ref[:, 0:N_META] = meta_ref[0]
        cvd_ref[...] = mvd_ref[0]

    def emit(z, carry_ref, out_ref):
        lane = lax.broadcasted_iota(jnp.int32, z.shape, 1)
        zr = pltpu.roll(z, N_META, 1)
        out_ref[0, 0] = jnp.where(lane < N_META, carry_ref[...], zr)
        carry_ref[...] = zr

    @pl.when(i < nt)
    def _():
        hb = _rms(x_ref[0], g_ref[...], NORM_EPS).astype(BF16)

        def mm(c):
            return _dot(hb, w_ref[:, c * HALF:(c + 1) * HALF])

        def mm_t(c):
            return _dot_nt(wt_ref[c * HALF:(c + 1) * HALF, :], hb)

        qd_ref[0] = (mm(0) * QSCALE).astype(BF16)
        z = mm(1)
        vd16_ref[0] = z.astype(BF16)
        for h in range(DIFF_HEADS):
            stage_ref[pl.ds(h, tm, stride=DIFF_HEADS), :] = z[:, h * GW:(h + 1) * GW]
        svd_ref[0, 0, 0:vtail, :] = cvd_ref[...]
        svd_ref[0, 0, vtail:vrows, :] = stage_ref[0:vrows - vtail, :]
        cvd_ref[...] = stage_ref[vrows - vtail:vrows, :]
        qf_ref[0] = (mm(2) * QSCALE).astype(BF16)
        vf16_ref[0] = mm(3).astype(BF16)
        gate_ref[0, :, 0:HALF] = mm(4).astype(BF16)
        gate_ref[0, :, HALF:MIX] = mm(5).astype(BF16)
        z = mm_t(0)
        kdt16_ref[0] = z.astype(BF16)
        emit(z, ckd_ref, skd_ref)
        z = mm_t(1)
        kft16_ref[0] = z.astype(BF16)
        emit(z, ckf_ref, skf_ref)
        emit(mm_t(2), cvf_ref, svf_ref)
        zf = _dot_nt(wft_ref[...], hb)
        lf = _log_sigmoid(zf[0:FOX_HEADS] + bft_ref[...])
        lft_ref[0] = lf
        emit(lf, clf_ref, slf_ref)

    @pl.when(i == nt)
    def _():
        for carry_ref, out_ref, _ in shifted:
            out_ref[0, 0] = carry_ref[...]
        svd_ref[0, 0, 0:vtail, :] = cvd_ref[...]


def _project_frames(layer, x, g, w_cols, w_t, w_ft, b_ft, metas, meta_vd, stacks):
    depth = g.shape[0]
    nb, t, _ = x.shape
    tm = ROW_TILE
    nt = t // tm
    last = nt - 1
    tp = N_META + t
    rows = lambda b, i: (b, jnp.minimum(i, last), 0)
    cols = lambda b, i: (b, 0, jnp.minimum(i, last))
    of_layer = lambda b, i: (layer, 0, 0)
    rm16 = (jax.ShapeDtypeStruct((nb, t, HALF), BF16), pl.BlockSpec((1, tm, HALF), rows))
    fm16 = (jax.ShapeDtypeStruct((nb, HALF, t), BF16), pl.BlockSpec((1, HALF, tm), cols))
    lft = (jax.ShapeDtypeStruct((nb, FOX_HEADS, t), F32), pl.BlockSpec((1, FOX_HEADS, tm), cols))
    gate = (jax.ShapeDtypeStruct((nb, t, MIX), BF16), pl.BlockSpec((1, tm, MIX), rows))
    stacked = [(jax.ShapeDtypeStruct((depth, nb, m.shape[1], tp), F32),
                pl.BlockSpec((1, 1, m.shape[1], tm), lambda b, i: (layer, b, 0, i))) for m in metas]
    stacked.append((jax.ShapeDtypeStruct((depth, nb, DIFF_HEADS * tp, GW), F32),
                    pl.BlockSpec((1, 1, DIFF_HEADS * tm, GW), lambda b, i: (layer, b, i, 0))))
    outs = (rm16, fm16, rm16, rm16, fm16, rm16, lft, gate, *stacked)
    aliased = () if stacks is None else tuple(stacks)
    first_stack_in = 7 + len(metas)
    first_stack_out = len(outs) - len(stacked)
    return pl.pallas_call(
        functools.partial(_proj_frames_kernel, nt, len(aliased)),
        out_shape=tuple(o[0] for o in outs),
        grid=(nb, nt + 1),
        in_specs=[pl.BlockSpec((1, tm, D_MODEL), rows),
                  pl.BlockSpec((None, 1, D_MODEL), of_layer),
                  pl.BlockSpec((None, D_MODEL, 6 * HALF), of_layer),
                  pl.BlockSpec((None, 3 * HALF, D_MODEL), of_layer),
                  pl.BlockSpec((None, 2 * FOX_HEADS, D_MODEL), of_layer),
                  pl.BlockSpec((None, FOX_HEADS, 1), of_layer),
                  *[pl.BlockSpec((1, m.shape[1], N_META), lambda b, i: (b, 0, 0)) for m in metas],
                  pl.BlockSpec((1, DIFF_HEADS * N_META, GW), lambda b, i: (b, 0, 0)),
                  *[pl.BlockSpec(memory_space=pl.ANY) for _ in aliased]],
        out_specs=tuple(o[1] for o in outs),
        scratch_shapes=[*[pltpu.VMEM((m.shape[1], tm), F32) for m in metas],
                        pltpu.VMEM((DIFF_HEADS * N_META, GW), F32),
                        pltpu.VMEM((DIFF_HEADS * tm, GW), F32)],
        input_output_aliases={first_stack_in + k: first_stack_out + k for k in range(len(aliased))},
        compiler_params=pltpu.CompilerParams(dimension_semantics=("arbitrary", "arbitrary"),
                                             vmem_limit_bytes=VMEM_LIMIT),
        name="proj_frames",
    )(x, g, w_cols, w_t, w_ft, b_ft, *metas, meta_vd, *aliased)


def _cumsum_kernel(xa_ref, xb_ref, oa_ref, ob_ref):
    rows = xa_ref.shape[0]
    tris = {}

    def tri(n):
        if n not in tris:
            r = lax.broadcasted_iota(jnp.int32, (n, n), 0)
            c = lax.broadcasted_iota(jnp.int32, (n, n), 1)
            tris[n] = (r <= c).astype(BF16)
        return tris[n]

    carry = jnp.zeros((rows, 1), F32)
    for x_ref, o_ref in ((xa_ref, oa_ref), (xb_ref, ob_ref)):
        cols = x_ref.shape[1]
        for start in range(0, cols, CUM_CHUNK):
            w = min(CUM_CHUNK, cols - start)
            x = x_ref[:, start:start + w]
            hi = x.astype(BF16)
            r1 = x - hi.astype(F32)
            mid = r1.astype(BF16)
            lo = (r1 - mid.astype(F32)).astype(BF16)
            y = (_dot(hi, tri(w)) + _dot(mid, tri(w))) + _dot(lo, tri(w)) + carry
            o_ref[:, start:start + w] = y
            carry = y[:, w - 1:w]


def _cum_logf(layer, lead, new):
    nb, h, lb = new.shape
    la = lead.shape[-1]
    rows = nb * h
    if lead.ndim == 4:
        lead = lead.reshape(lead.shape[0], rows, la)
        lead_spec = pl.BlockSpec((None, rows, la), lambda i: (layer, 0, 0))
    else:
        lead = lead.reshape(rows, la)
        lead_spec = pl.BlockSpec((rows, la), lambda i: (0, 0))
    fa, fb = pl.pallas_call(
        _cumsum_kernel,
        out_shape=(jax.ShapeDtypeStruct((rows, la), F32), jax.ShapeDtypeStruct((rows, lb), F32)),
        grid=(1,),
        in_specs=[lead_spec, pl.BlockSpec((rows, lb), lambda i: (0, 0))],
        out_specs=(pl.BlockSpec((rows, la), lambda i: (0, 0)), pl.BlockSpec((rows, lb), lambda i: (0, 0))),
        compiler_params=pltpu.CompilerParams(dimension_semantics=("arbitrary",),
                                             vmem_limit_bytes=VMEM_LIMIT),
        name="cumsum",
    )(lead, new.reshape(rows, lb))
    return fa.reshape(nb, h, la), fb.reshape(nb, h, lb)


def _lambda(lam_ref, lam_init):
    lv = lam_ref[...]
    a = jnp.sum(lv[0:1] * lv[1:2], axis=-1, keepdims=True)
    b = jnp.sum(lv[2:3] * lv[3:4], axis=-1, keepdims=True)
    return jnp.exp(a) - jnp.exp(b) + lam_init


def _pick_row(x, r):
    row = lax.broadcasted_iota(jnp.int32, x.shape, 0)
    return jnp.sum(jnp.where(row == r, x, 0.0), axis=0, keepdims=True)


def _split_lanes(x, fill):
    lane = lax.broadcasted_iota(jnp.int32, x.shape, 1)
    f = jnp.full_like(x, fill)
    return jnp.where(lane < HEAD_DIM, x, f), jnp.where(lane >= HEAD_DIM, x, f)


def _diff_out(o1, o2, lam, lam_init, g_sub):
    return _rms(o1 - lam * o2, g_sub, SUBLN_EPS) * (1.0 - lam_init)


def _frames_attn_kernel(kind, tq, lam_init, *refs):
    if kind == "diff":
        (sc_ref, lam_ref, q_ref, kt_ref, v_ref, mk_ref, mv_ref, g_ref, o_ref,
         sa_ref, sb_ref, acc_ref, diag_ref) = refs
    else:
        q_ref, kt_ref, v_ref, mk_ref, mv_ref, fk_ref, fm_ref, o_ref, sa_ref, sb_ref, acc_ref, diag_ref = refs
    gidx = pl.program_id(1)
    qi = pl.program_id(2)
    qs = _split_lanes(q_ref[0], 0)
    s_refs = (sa_ref, sb_ref)
    if kind == "diff":
        slope = sc_ref[gidx]

    @pl.when(qi == 0)
    def _():
        qr = lax.broadcasted_iota(jnp.int32, (tq, tq), 0)
        kr = lax.broadcasted_iota(jnp.int32, (tq, tq), 1)
        if kind == "diff":
            tile = slope * (2 * jnp.minimum(qr, kr) - kr).astype(F32)
            mask = (kr >> CHUNK_SHIFT) <= (qr >> CHUNK_SHIFT)
        else:
            tile = jnp.zeros((tq, tq), F32)
            mask = kr <= qr
        diag_ref[...] = jnp.where(mask, tile, NEG)

    def values(v):
        if kind == "diff":
            ve = jnp.concatenate([v, jnp.ones_like(v)], axis=1)
            return ve, ve
        return _split_lanes(v, 1)

    def produce(n, slot, diag):
        ks = pl.multiple_of(n * tq, tq)
        ktb = kt_ref[0, :, pl.ds(ks, tq)]
        if kind == "diff":
            if diag:
                bias = [diag_ref[...]] * 2
            else:
                kpos = lax.broadcasted_iota(jnp.int32, (1, tq), 1) + (n - qi) * tq
                bias = [slope * kpos.astype(F32)] * 2
        else:
            fk = fk_ref[0, :, pl.ds(ks, tq)]
            bias = [-LOG2E * _pick_row(fk, 2 * gidx), -LOG2E * _pick_row(fk, 2 * gidx + 1)]
        bms = []
        for i in range(2):
            s = _dot(qs[i], ktb) + bias[i]
            if diag and kind == "fox":
                s = s + diag_ref[...]
            s_refs[slot][i] = s
            bms.append(jnp.max(s, axis=-1, keepdims=True))
        return tuple(bms)

    def consume(n, slot, ms, bms):
        ks = pl.multiple_of(n * tq, tq)
        vs = values(v_ref[0, pl.ds(ks, tq), :])
        out = []
        for i in range(2):
            m_new = jnp.maximum(ms[i], bms[i])
            alpha = jnp.exp2(ms[i] - m_new)
            p = jnp.exp2(s_refs[slot][i] - m_new)
            acc_ref[i] = alpha * acc_ref[i] + _dot(p.astype(BF16), vs[i])
            out.append(m_new)
        return tuple(out)

    mv = mv_ref[0]
    if kind == "diff":
        kpos = lax.broadcasted_iota(jnp.int32, (1, N_META), 1) - (N_META + qi * tq)
        bias_m = [slope * kpos.astype(F32)] * 2
    else:
        fm = fm_ref[0]
        bias_m = [-LOG2E * _pick_row(fm, 2 * gidx), -LOG2E * _pick_row(fm, 2 * gidx + 1)]
    s = _dot_nt(jnp.concatenate(qs, axis=0), mk_ref[0])
    ms, ps = [], []
    for i in range(2):
        si = s[i * tq:(i + 1) * tq] + bias_m[i]
        m = jnp.max(si, axis=-1, keepdims=True)
        ps.append(jnp.exp2(si - m).astype(BF16))
        ms.append(m)
    r = _dot(jnp.concatenate(ps, axis=0), jnp.concatenate([mv, jnp.ones_like(mv)], axis=1))
    if kind == "diff":
        acc_ref[0] = r[:tq]
        acc_ref[1] = r[tq:]
    else:
        lane = lax.broadcasted_iota(jnp.int32, (tq, GW), 1)
        acc_ref[0] = jnp.where(lane < HEAD_DIM, r[:tq, :GW], r[:tq, GW:])
        acc_ref[1] = jnp.where(lane >= HEAD_DIM, r[tq:, :GW], r[tq:, GW:])
    ms = tuple(ms)

    bm0 = produce(qi, 0, True)

    def pair(t, c):
        ms, bm0 = c[:2], c[2:]
        bm1 = produce(2 * t, 1, False)
        ms = consume(jnp.where(t == 0, qi, 2 * t - 1), 0, ms, bm0)
        bm0 = produce(2 * t + 1, 0, False)
        ms = consume(2 * t, 1, ms, bm1)
        return ms + bm0

    c = lax.fori_loop(0, qi // 2, pair, ms + bm0)
    ms, bm0 = c[:2], c[2:]
    pending = jnp.where(qi < 2, qi, 2 * (qi // 2) - 1)

    def odd(ms, bm0):
        bm1 = produce(qi - 1, 1, False)
        ms = consume(pending, 0, ms, bm0)
        return consume(qi - 1, 1, ms, bm1)

    def even(ms, bm0):
        return consume(pending, 0, ms, bm0)

    lax.cond(qi % 2 == 1, odd, even, ms, bm0)

    a = acc_ref[0]
    b = acc_ref[1]
    if kind == "diff":
        out = _diff_out(a[:, :GW] / a[:, GW:], b[:, :GW] / b[:, GW:],
                        _lambda(lam_ref, lam_init), lam_init, g_ref[...])
    else:
        lane = lax.broadcasted_iota(jnp.int32, a.shape, 1)
        out = jnp.where(lane < HEAD_DIM, a / pltpu.roll(a, HEAD_DIM, 1), b / pltpu.roll(b, HEAD_DIM, 1))
    o_ref[0] = out.astype(o_ref.dtype)


def _frames_attention(kind, lam_init, q, kt, v, mk, mv, extra):
    nb, t, _ = q.shape
    tq = ATTN_BLOCK
    idx = lambda b, g, i: (b, i, g)
    rows_g = lambda b, g, i: (b, 0, g)
    feat_g = lambda b, g, i: (b, g, 0)
    fixed = lambda b, g, i: (0, 0)
    qspec = pl.BlockSpec((1, tq, GW), idx)
    common = [qspec, pl.BlockSpec((1, GW, t), feat_g), pl.BlockSpec((1, t, GW), rows_g),
              pl.BlockSpec((1, N_META, GW), rows_g), pl.BlockSpec((1, N_META, GW), rows_g)]
    if kind == "diff":
        slopes, lam_vecs, g_sub = extra
        args = (slopes, lam_vecs, q, kt, v, mk, mv, g_sub)
        in_specs = [pl.BlockSpec(memory_space=pltpu.SMEM), pl.BlockSpec((4, HEAD_DIM), fixed),
                    *common, pl.BlockSpec((1, GW), fixed)]
        acc_w = 2 * GW
    else:
        f_frames, f_meta = extra
        args = (q, kt, v, mk, mv, f_frames, f_meta)
        in_specs = [*common,
                    pl.BlockSpec((1, FOX_HEADS, t), lambda b, g, i: (b, 0, 0)),
                    pl.BlockSpec((1, FOX_HEADS, N_META), lambda b, g, i: (b, 0, 0))]
        acc_w = GW
    return pl.pallas_call(
        functools.partial(_frames_attn_kernel, kind, tq, lam_init),
        out_shape=jax.ShapeDtypeStruct((nb, t, HALF), BF16),
        grid=(nb, GROUPS, t // tq),
        in_specs=in_specs,
        out_specs=qspec,
        scratch_shapes=[pltpu.VMEM((2, tq, tq), F32), pltpu.VMEM((2, tq, tq), F32),
                        pltpu.VMEM((2, tq, acc_w), F32), pltpu.VMEM((tq, tq), F32)],
        compiler_params=pltpu.CompilerParams(
            dimension_semantics=("arbitrary", "arbitrary", "arbitrary"),
            vmem_limit_bytes=VMEM_LIMIT),
        name="frames_attn_" + kind,
    )(*args)


def _small_attn_kernel(kind, tq, lc, ln, lam_init, *refs):
    refs = list(refs)
    o_ref = refs.pop()
    if kind == "diff":
        sc_ref, lam_ref = refs[0:2]
        refs = refs[2:]
    q_ref = refs.pop(0)
    if lc:
        kc_ref, vc_ref = refs[0:2]
        refs = refs[2:]
    kn_ref, vn_ref = refs[0:2]
    refs = refs[2:]
    if kind == "diff":
        g_ref, = refs
        lam = _lambda(lam_ref, lam_init)
        g_sub = g_ref[...]
    else:
        if lc:
            fc_ref = refs.pop(0)
        fn_ref, = refs

    lc_main = (lc // LANES) * LANES
    spans = [sp for sp in ((0, lc_main), (lc_main, lc)) if sp[1] > sp[0]]
    qpos = lc + lax.broadcasted_iota(jnp.int32, (tq, 1), 0)

    for g in range(GROUPS):
        cols = slice(g * GW, (g + 1) * GW)
        qs = _split_lanes(q_ref[0, :, cols], 0)
        segs = []
        for a, b in spans:
            kt = kc_ref[0, 0, cols, a:b].astype(BF16)
            if kind == "diff":
                vb = vc_ref[0, 0, pl.ds(a * DIFF_HEADS + g, b - a, stride=DIFF_HEADS), :].astype(BF16)
                pv = functools.partial(lambda p, vb: _dot(p, vb), vb=vb)
            else:
                vt = vc_ref[0, 0, cols, a:b].astype(BF16)
                pv = functools.partial(lambda p, vt: _dot_nt(p, vt), vt=vt)
            f = fc_ref[0, 2 * g:2 * g + 2, a:b] if kind == "fox" else None
            segs.append((a, b - a, f, functools.partial(lambda q, kt: _dot(q, kt), kt=kt), pv))
        kn = kn_ref[0, :, cols]
        vn = vn_ref[0, :, cols]
        f = fn_ref[0, 2 * g:2 * g + 2, :] if kind == "fox" else None
        segs.append((lc, ln, f, functools.partial(lambda q, kn: _dot_nt(q, kn), kn=kn),
                     functools.partial(lambda p, vn: _dot(p, vn), vn=vn)))

        s_cat = [seg[3](jnp.concatenate(qs, axis=0)) for seg in segs]
        ps, ls = [], []
        for i in range(2):
            ss = []
            for (start, length, f, _, _), sc in zip(segs, s_cat):
                kpos = start + lax.broadcasted_iota(jnp.int32, (1, length), 1)
                s = sc[i * tq:(i + 1) * tq]
                if kind == "diff":
                    s = s - sc_ref[g] * jnp.abs(qpos - kpos).astype(F32)
                    mask = ((kpos - N_META) >> CHUNK_SHIFT) <= ((qpos - N_META) >> CHUNK_SHIFT)
                else:
                    s = s - LOG2E * f[i:i + 1]
                    mask = kpos <= qpos
                ss.append(jnp.where(mask, s, NEG))
            m = functools.reduce(jnp.maximum, [jnp.max(s, axis=-1, keepdims=True) for s in ss])
            ps.append([jnp.exp2(s - m) for s in ss])
            ls.append(functools.reduce(jnp.add, [jnp.sum(p, axis=-1, keepdims=True) for p in ps[i]]))
        acc = functools.reduce(jnp.add, [seg[4](jnp.concatenate([pa, pb], axis=0).astype(BF16))
                                         for pa, pb, seg in zip(ps[0], ps[1], segs)])
        os_ = [acc[i * tq:(i + 1) * tq] / ls[i] for i in range(2)]
        if kind == "diff":
            out = _diff_out(os_[0], os_[1], lam, lam_init, g_sub)
        else:
            lane = lax.broadcasted_iota(jnp.int32, os_[0].shape, 1)
            out = jnp.where(lane < HEAD_DIM, os_[0], os_[1])
        o_ref[0, :, cols] = out.astype(o_ref.dtype)


def _small_attention(kind, lam_init, layer, q, kc, vc, kn, vn, extra):
    nb, tq, _ = q.shape
    lc = 0 if kc is None else kc.shape[-1]
    ln = kn.shape[1]
    per_b = lambda rows: pl.BlockSpec((1, rows, HALF), lambda b: (b, 0, 0))
    cache = lambda a: pl.BlockSpec((1, 1) + a.shape[2:], lambda b: (layer, b, 0, 0))
    args, in_specs = [], []
    if kind == "diff":
        slopes, lam_vecs, g_sub = extra
        args += [slopes, lam_vecs]
        in_specs += [pl.BlockSpec(memory_space=pltpu.SMEM),
                     pl.BlockSpec((4, HEAD_DIM), lambda b: (0, 0))]
    args.append(q)
    in_specs.append(per_b(tq))
    if lc:
        args += [kc, vc]
        in_specs += [cache(kc), cache(vc)]
    args += [kn, vn]
    in_specs += [per_b(ln), per_b(ln)]
    if kind == "diff":
        args.append(g_sub)
        in_specs.append(pl.BlockSpec((1, GW), lambda b: (0, 0)))
    else:
        f_cache, f_new = extra
        if lc:
            args.append(f_cache)
            in_specs.append(pl.BlockSpec((1, FOX_HEADS, lc), lambda b: (b, 0, 0)))
        args.append(f_new)
        in_specs.append(pl.BlockSpec((1, FOX_HEADS, ln), lambda b: (b, 0, 0)))
    return pl.pallas_call(
        functools.partial(_small_attn_kernel, kind, tq, lc, ln, lam_init),
        out_shape=jax.ShapeDtypeStruct((nb, tq, HALF), BF16),
        grid=(nb,),
        in_specs=in_specs,
        out_specs=per_b(tq),
        compiler_params=pltpu.CompilerParams(dimension_semantics=("arbitrary",),
                                             vmem_limit_bytes=VMEM_LIMIT),
        name="small_attn_" + kind,
    )(*args)


def _outproj_kernel(final, ad_ref, af_ref, gate_ref, x_ref, w_ref, gf_ref, y_ref):
    gt = gate_ref[...].astype(F32)
    sg = gt / (1.0 + jnp.exp(-gt))
    ud = (ad_ref[...].astype(F32) * sg[:, 0:HALF]).astype(BF16)
    uf = (af_ref[...].astype(F32) * sg[:, HALF:MIX]).astype(BF16)
    y = x_ref[...] + (_dot(ud, w_ref[0:HALF, :]) + _dot(uf, w_ref[HALF:MIX, :]))
    if final:
        y = _rms(y, gf_ref[...], NORM_EPS)
    y_ref[...] = y


def _out_project(final, layer, ad, af, gate, x, w_out, g_final):
    rows = x.shape[0]
    tm = math.gcd(rows, ROW_TILE)
    row = lambda i: (i, 0)
    fixed = lambda i: (0, 0)
    return pl.pallas_call(
        functools.partial(_outproj_kernel, final),
        out_shape=jax.ShapeDtypeStruct((rows, D_MODEL), F32),
        grid=(rows // tm,),
        in_specs=[pl.BlockSpec((tm, HALF), row), pl.BlockSpec((tm, HALF), row),
                  pl.BlockSpec((tm, MIX), row), pl.BlockSpec((tm, D_MODEL), row),
                  pl.BlockSpec((None, MIX, D_MODEL), lambda i: (layer, 0, 0)),
                  pl.BlockSpec((1, D_MODEL), fixed)],
        out_specs=pl.BlockSpec((tm, D_MODEL), row),
        compiler_params=pltpu.CompilerParams(dimension_semantics=("arbitrary",),
                                             vmem_limit_bytes=VMEM_LIMIT),
        name="outproj",
    )(ad, af, gate, x, w_out, g_final)


def kernel(x_prompt, x_sample, cache_diff_k, cache_diff_v, cache_fox_k, cache_fox_v, cache_fox_logf,
           meta_tokens, w_in, b_forget, norm_g, w_out, lambda_q1, lambda_k1, lambda_q2, lambda_k2,
           subln_g, final_norm_g):
    depth = w_in.shape[0]
    nb_p, seq, _ = x_prompt.shape
    nb_s, seq_s, _ = x_sample.shape
    lc = cache_diff_k.shape[2]

    c = [n * HALF for n in range(7)]
    fl0, g0 = c[6], c[6] + FOX_HEADS
    w_rows = jnp.concatenate([w_in[:, :, :fl0], w_in[:, :, g0:]], axis=-1).astype(BF16)
    w_f = jnp.pad(w_in[:, :, fl0:g0], ((0, 0), (0, 0), (0, LANES - FOX_HEADS))).astype(BF16)
    b_f = jnp.pad(b_forget.astype(F32), ((0, 0), (0, LANES - FOX_HEADS)))[:, None, :]
    w_cols = jnp.concatenate([w_in[:, :, c[0]:c[1]], w_in[:, :, c[2]:c[4]], w_in[:, :, c[5]:c[6]],
                              w_in[:, :, g0:]], axis=-1).astype(BF16)
    w_in_t = jnp.swapaxes(w_in, 1, 2)
    w_t = jnp.concatenate([w_in_t[:, c[1]:c[2]], w_in_t[:, c[4]:c[6]]], axis=1).astype(BF16)
    w_ft = jnp.pad(w_in_t[:, fl0:g0], ((0, 0), (0, FOX_HEADS), (0, 0))).astype(BF16)
    b_ft = b_forget.astype(F32)[:, :, None]
    w_o = w_out.astype(BF16)
    slopes = (2.0 ** (-8.0 * jnp.arange(1, DIFF_HEADS + 1, dtype=F32) / DIFF_HEADS)) * LOG2E
    g_final = final_norm_g.astype(F32).reshape(1, D_MODEL)
    g_layers = norm_g.astype(F32)[:, None, :]

    ckd_t = jnp.transpose(cache_diff_k, (0, 1, 3, 4, 5, 2)).reshape(depth, nb_s, HALF, lc)
    cvd = cache_diff_v.reshape(depth, nb_s, lc * DIFF_HEADS, GW)
    ckf_t = jnp.transpose(cache_fox_k, (0, 1, 3, 4, 2)).reshape(depth, nb_s, HALF, lc)
    cvf_t = jnp.transpose(cache_fox_v, (0, 1, 3, 4, 2)).reshape(depth, nb_s, HALF, lc)
    clf_t = jnp.swapaxes(cache_fox_logf, 2, 3).astype(F32)

    xf = x_prompt
    xm = jnp.broadcast_to(meta_tokens[None].astype(x_prompt.dtype),
                          (nb_p, N_META, D_MODEL)).reshape(nb_p * N_META, D_MODEL)
    xs = x_sample.reshape(nb_s * seq_s, D_MODEL)

    tp = N_META + seq
    stacks = (*(jnp.zeros((depth, nb_p, r, tp), F32) for r in (HALF, HALF, HALF, FOX_HEADS)),
              jnp.zeros((depth, nb_p, DIFF_HEADS * tp, GW), F32))
    rows_s = [[] for _ in range(5)]
    shape3 = lambda a, nb: a.reshape(nb, -1, a.shape[-1])
    for l in range(depth):
        lam_init = 0.8 - 0.6 * math.exp(-0.3 * l)
        final = l == depth - 1
        lam_vecs = jnp.stack([lambda_q1[l], lambda_k1[l], lambda_q2[l], lambda_k2[l]]).astype(F32)
        diff_extra = (slopes, lam_vecs, subln_g[l].astype(F32).reshape(1, GW))

        def out_project(ad, af, gate, x):
            return _out_project(final, l, ad.reshape(-1, HALF), af.reshape(-1, HALF), gate.reshape(-1, MIX),
                                x.reshape(-1, D_MODEL), w_o, g_final)

        (mqd, mkd32, mkd16, mvd32, mvd16, mqf, mkf32, mkf16, mvf32, mvf16, mlf, mgate) = [
            shape3(a, nb_p) for a in _project_rows(l, xm, g_layers, w_rows, w_f, b_f)]
        mlf_t = jnp.swapaxes(mlf[:, :, :FOX_HEADS], 1, 2)
        metas = (jnp.swapaxes(mkd32, 1, 2), jnp.swapaxes(mkf32, 1, 2), jnp.swapaxes(mvf32, 1, 2), mlf_t)
        (qd, kdt16, vd16, qf, kft16, vf16, lft, gate, *stacks) = _project_frames(
            l, xf, g_layers, w_cols, w_t, w_ft, b_ft, metas,
            mvd32.reshape(nb_p, DIFF_HEADS * N_META, GW), stacks)
        f_meta, f_frames = _cum_logf(l, mlf_t, lft)

        ad = _frames_attention("diff", lam_init, qd, kdt16, vd16, mkd16, mvd16, diff_extra)
        af = _frames_attention("fox", lam_init, qf, kft16, vf16, mkf16, mvf16, (f_frames, f_meta))
        xf = out_project(ad, af, gate, xf).reshape(nb_p, seq, D_MODEL)
        if not final:
            mad = _small_attention("diff", lam_init, l, mqd, None, None, mkd16, mvd16, diff_extra)
            maf = _small_attention("fox", lam_init, l, mqf, None, None, mkf16, mvf16, (None, f_meta))
            xm = out_project(mad, maf, mgate, xm)

        (sqd, skd32, skd16, svd32, svd16, sqf, skf32, skf16, svf32, svf16, slf, sgate) = [
            shape3(a, nb_s) for a in _project_rows(l, xs, g_layers, w_rows, w_f, b_f)]
        logf_s = slf[:, :, :FOX_HEADS]
        f_s = _cum_logf(l, clf_t, jnp.swapaxes(logf_s, 1, 2))
        sad = _small_attention("diff", lam_init, l, sqd, ckd_t, cvd, skd16, svd16, diff_extra)
        saf = _small_attention("fox", lam_init, l, sqf, ckf_t, cvf_t, skf16, svf16, f_s)
        xs = out_project(sad, saf, sgate, xs)
        for dst, a in zip(rows_s, (skd32, svd32, skf32, svf32, logf_s)):
            dst.append(a)

    y_prompt = xf
    y_sample = xs.reshape(nb_s, seq_s, D_MODEL)
    skd, skf, svf, slf_stack, svd = stacks
    p_diff_k = jnp.transpose(skd.reshape(depth, nb_p, DIFF_HEADS, 2, HEAD_DIM, tp), (0, 1, 5, 2, 3, 4))
    p_diff_v = svd.reshape(depth, nb_p, tp, DIFF_HEADS, 2 * HEAD_DIM)
    p_fox_k = jnp.transpose(skf.reshape(depth, nb_p, FOX_HEADS, HEAD_DIM, tp), (0, 1, 4, 2, 3))
    p_fox_v = jnp.transpose(svf.reshape(depth, nb_p, FOX_HEADS, HEAD_DIM, tp), (0, 1, 4, 2, 3))
    p_fox_logf = jnp.swapaxes(slf_stack, 2, 3)
    s_diff_k = jnp.stack(rows_s[0]).reshape(depth, nb_s, seq_s, DIFF_HEADS, 2, HEAD_DIM)
    s_diff_v = jnp.stack(rows_s[1]).reshape(depth, nb_s, seq_s, DIFF_HEADS, 2 * HEAD_DIM)
    s_fox_k = jnp.stack(rows_s[2]).reshape(depth, nb_s, seq_s, FOX_HEADS, HEAD_DIM)
    s_fox_v = jnp.stack(rows_s[3]).reshape(depth, nb_s, seq_s, FOX_HEADS, HEAD_DIM)
    s_fox_logf = jnp.stack(rows_s[4])
    return (y_prompt, y_sample, p_diff_k, p_diff_v, p_fox_k, p_fox_v, p_fox_logf,
            s_diff_k, s_diff_v, s_fox_k, s_fox_v, s_fox_logf)
```
